```python
import math
import jax, jax.numpy as jnp
from jax import lax
import numpy as np

D_MODEL = 1024
BATCH = 4
SEQ = 4096
DEPTH = 4
DEC_BATCH = 128
DEC_SEQ = 4
PAST_LEN = 2048
PAGE_SIZE = 128

MIX_WIDTH = D_MODEL
GLA_VAL = D_MODEL // 4
GLA_HEADS = 4
GLA_DV = GLA_VAL // GLA_HEADS
GLA_DK = GLA_DV // 2
GLA_KEY = GLA_HEADS * GLA_DK
GATE_RANK = 16
GATE_TAU = 16.0
GLA_CHUNK = 64
POOL_WIDTH = D_MODEL // 4
POOL_WINDOWS = (2, 4, 8, 16)
POOL_GROUPS = len(POOL_WINDOWS)
POOL_GROUP_DIM = POOL_WIDTH // POOL_GROUPS
POOL_BUF = max(POOL_WINDOWS) - 1
SB_WIDTH = D_MODEL // 2
SB_DIM = 64
SB_HEADS = SB_WIDTH // SB_DIM
SB_BLOCK = 128
SB_BIAS_INIT = -6.0
IN_SIZES = (GLA_KEY, GLA_KEY, GLA_VAL, GATE_RANK, GLA_VAL, POOL_WIDTH, SB_WIDTH, SB_WIDTH, SB_WIDTH)
IN_COLS = sum(IN_SIZES)
FFN_HIDDEN = -(-8 * D_MODEL // (3 * 256)) * 256
DEEPNORM_ALPHA = (2.0 * DEPTH) ** 0.25
DEEPNORM_BETA = (8.0 * DEPTH) ** -0.25
LN_EPS = 1e-5

kernel_name = "hybrid_gla_pool_stickbreak_decoder_step"


def layer_norm(x, g, b):
    xf = x.astype(jnp.float32)
    mu = jnp.mean(xf, axis=-1, keepdims=True)
    var = jnp.mean(jnp.square(xf - mu), axis=-1, keepdims=True)
    return ((xf - mu) * lax.rsqrt(var + LN_EPS) * g + b).astype(x.dtype)


def split_cols(proj):
    outs, start = [], 0
    for size in IN_SIZES:
        outs.append(proj[..., start:start + size])
        start += size
    return outs


def gla_chunked(q, k, v, log_a, s0):
    B, T, H, _ = q.shape
    DV = v.shape[-1]
    L = GLA_CHUNK
    pad = (-T) % L
    n = (T + pad) // L

    def prep(a):
        a = jnp.pad(a.astype(jnp.float32), ((0, 0), (0, pad), (0, 0), (0, 0)))
        return a.reshape(B, n, L, H, a.shape[-1]).transpose(1, 0, 2, 3, 4)

    qc, kc, vc, lc = prep(q), prep(k), prep(v), prep(log_a)
    causal = jnp.tril(jnp.ones((L, L), dtype=bool))

    def step(S, inp):
        qi, ki, vi, li = inp
        b = jnp.cumsum(li, axis=1)
        b_last = b[:, -1]
        b_mid = b[:, L // 2 - 1:L // 2]
        o_inter = jnp.einsum('blhk,bhkv->blhv', qi * jnp.exp(b), S)
        scores = jnp.einsum('blhk,bshk->bhls', qi * jnp.exp(b - b_mid), ki * jnp.exp(b_mid - b))
        scores = jnp.where(causal, scores, 0.0)
        o_intra = jnp.einsum('bhls,bshv->blhv', scores, vi)
        S = jnp.exp(b_last)[..., None] * S + jnp.einsum(
            'bshk,bshv->bhkv', ki * jnp.exp(b_last[:, None] - b), vi)
        return S, o_inter + o_intra

    S, o = lax.scan(step, s0.astype(jnp.float32), (qc, kc, vc, lc))
    o = o.transpose(1, 0, 2, 3, 4).reshape(B, n * L, H, DV)[:, :T]
    return o, S


def head_norm(o, g):
    mu = jnp.mean(o, axis=-1, keepdims=True)
    var = jnp.mean(jnp.square(o - mu), axis=-1, keepdims=True)
    return (o - mu) * lax.rsqrt(var + LN_EPS) * g.astype(jnp.float32).reshape(GLA_HEADS, GLA_DV)


def pool_mix(u, buf, pos0, w_pool, scale):
    T = u.shape[1]
    P = POOL_BUF
    ext = jnp.concatenate([buf.astype(jnp.float32), u.astype(jnp.float32)], axis=1)
    cs = jnp.pad(jnp.cumsum(ext, axis=1), ((0, 0), (1, 0), (0, 0)))
    pos = pos0 + jnp.arange(T)
    outs = []
    for g, w in enumerate(POOL_WINDOWS):
        c0, c1 = g * POOL_GROUP_DIM, (g + 1) * POOL_GROUP_DIM
        hi = cs[:, P + 1:P + 1 + T, c0:c1]
        lo = cs[:, P + 1 - w:P + 1 - w + T, c0:c1]
        cnt = jnp.minimum(pos + 1, w).astype(jnp.float32)[None, :, None]
        d = (hi - lo) / cnt - ext[:, P:, c0:c1]
        outs.append(jnp.einsum('btc,cd->btd', d, w_pool[g].astype(jnp.float32)))
    y = jnp.concatenate(outs, axis=-1) * scale.astype(jnp.float32)
    return y.astype(u.dtype), ext[:, -P:].astype(u.dtype)


def sb_attend(q, k, v, bias, q_pos, k_pos):
    z = (jnp.einsum('bqhd,bkhd->bhqk', q, k).astype(jnp.float32) * (SB_DIM ** -0.5)
         + bias.astype(jnp.float32)[None, :, None, None])
    mask = k_pos[None, :] < q_pos[:, None]
    log_beta = jax.nn.log_sigmoid(z)
    log_1m = jnp.where(mask, jax.nn.log_sigmoid(-z), 0.0)
    between = lax.cumsum(log_1m, axis=3, reverse=True) - log_1m
    A = jnp.where(mask, jnp.exp(log_beta + between), 0.0)
    return jnp.einsum('bhqk,bkhd->bqhd', A.astype(v.dtype), v)


def sb_prompt(q, k, v, bias):
    B, T, H, D = q.shape
    nb = T // SB_BLOCK
    qb = q.reshape(B, nb, SB_BLOCK, H, D).transpose(1, 0, 2, 3, 4)
    k_pos = jnp.arange(T)

    def blk(args):
        qi, i = args
        q_pos = i * SB_BLOCK + jnp.arange(SB_BLOCK)
        return sb_attend(qi, k, v, bias, q_pos, k_pos)

    ob = lax.map(blk, (qb, jnp.arange(nb)))
    return ob.transpose(1, 0, 2, 3, 4).reshape(B, T, H, D)


def layer_forward(x, gla_s0, pool_buf0, pos0, attend,
                  w_in, w_gate_up, b_gate, gla_norm_g, w_pool, pool_scale, sb_bias, w_out,
                  ln1_g, ln1_b, w_ffn_in, w_ffn_out, ln2_g, ln2_b):
    B, T, _ = x.shape
    proj = jnp.einsum('btd,dc->btc', x, w_in)
    q_g, k_g, v_g, a_lr, g_g, u, q_s, k_s, v_s = split_cols(proj)
    z = jnp.einsum('btr,rk->btk', a_lr, w_gate_up) + b_gate
    log_a = jax.nn.log_sigmoid(z.astype(jnp.float32)) / GATE_TAU
    qh = q_g.reshape(B, T, GLA_HEADS, GLA_DK) * (GLA_DK ** -0.5)
    kh = k_g.reshape(B, T, GLA_HEADS, GLA_DK)
    vh = v_g.reshape(B, T, GLA_HEADS, GLA_DV)
    o_gla, gla_s = gla_chunked(qh, kh, vh, log_a.reshape(B, T, GLA_HEADS, GLA_DK), gla_s0)
    o_gla = (head_norm(o_gla, gla_norm_g).reshape(B, T, GLA_VAL)
             * jax.nn.silu(g_g.astype(jnp.float32))).astype(x.dtype)
    o_pool, pool_buf = pool_mix(u, pool_buf0, pos0, w_pool, pool_scale)
    qs = q_s.reshape(B, T, SB_HEADS, SB_DIM)
    ks = k_s.reshape(B, T, SB_HEADS, SB_DIM)
    vs = v_s.reshape(B, T, SB_HEADS, SB_DIM)
    o_sb = attend(qs, ks, vs, sb_bias).reshape(B, T, SB_WIDTH)
    mix = jnp.einsum('btc,cd->btd', jnp.concatenate([o_gla, o_pool, o_sb], axis=-1), w_out)
    x = layer_norm(DEEPNORM_ALPHA * x + mix, ln1_g, ln1_b)
    h = jnp.einsum('btd,df->btf', x, w_ffn_in)
    gate, up = h[..., :FFN_HIDDEN], h[..., FFN_HIDDEN:]
    ffn = jnp.einsum('btf,fd->btd', jax.nn.silu(gate) * up, w_ffn_out)
    x = layer_norm(DEEPNORM_ALPHA * x + ffn, ln2_g, ln2_b)
    return x, gla_s.astype(gla_s0.dtype), pool_buf, ks, vs


def setup_inputs(seed: int = 0) -> dict:
    key = jax.random.key(seed)
    ks = jax.random.split(key, 24)
    n_pages = PAST_LEN // PAGE_SIZE
    n_used = DEC_BATCH * n_pages
    n_phys = n_used + n_used // 4
    f32 = jnp.float32
    nrm = lambda k, shape, s: jax.random.normal(k, shape, f32) * s
    page_table = jax.random.permutation(ks[6], n_phys)[:n_used].reshape(DEC_BATCH, n_pages).astype(jnp.int32)
    return {
        "x_prompt": nrm(ks[0], (BATCH, SEQ, D_MODEL), 1.0),
        "x_sample": nrm(ks[1], (DEC_BATCH, DEC_SEQ, D_MODEL), 1.0),
        "state_gla": nrm(ks[2], (DEPTH, DEC_BATCH, GLA_HEADS, GLA_DK, GLA_DV), 0.5),
        "state_pool": nrm(ks[3], (DEPTH, DEC_BATCH, POOL_BUF, POOL_WIDTH), 1.0),
        "cache_k": nrm(ks[4], (DEPTH, n_phys, PAGE_SIZE, SB_HEADS, SB_DIM), 1.0),
        "cache_v": nrm(ks[5], (DEPTH, n_phys, PAGE_SIZE, SB_HEADS, SB_DIM), 1.0),
        "page_table": page_table,
        "w_in": nrm(ks[7], (DEPTH, D_MODEL, IN_COLS), D_MODEL ** -0.5),
        "w_gate_up": nrm(ks[8], (DEPTH, GATE_RANK, GLA_KEY), GATE_RANK ** -0.5),
        "b_gate": nrm(ks[9], (DEPTH, GLA_KEY), 0.01),
        "gla_norm_g": 1.0 + nrm(ks[10], (DEPTH, GLA_VAL), 0.02),
        "w_pool": nrm(ks[11], (DEPTH, POOL_GROUPS, POOL_GROUP_DIM, POOL_GROUP_DIM), POOL_GROUP_DIM ** -0.5),
        "pool_scale": 1.0 + nrm(ks[12], (DEPTH, POOL_WIDTH), 0.02),
        "sb_bias": SB_BIAS_INIT + nrm(ks[20], (DEPTH, SB_HEADS), 0.1),
        "w_out": nrm(ks[13], (DEPTH, MIX_WIDTH, D_MODEL), DEEPNORM_BETA * MIX_WIDTH ** -0.5),
        "ln1_g": 1.0 + nrm(ks[14], (DEPTH, D_MODEL), 0.02),
        "ln1_b": nrm(ks[15], (DEPTH, D_MODEL), 0.02),
        "w_ffn_in": nrm(ks[16], (DEPTH, D_MODEL, 2 * FFN_HIDDEN), D_MODEL ** -0.5),
        "w_ffn_out": nrm(ks[17], (DEPTH, FFN_HIDDEN, D_MODEL), DEEPNORM_BETA * FFN_HIDDEN ** -0.5),
        "ln2_g": 1.0 + nrm(ks[18], (DEPTH, D_MODEL), 0.02),
        "ln2_b": nrm(ks[19], (DEPTH, D_MODEL), 0.02),
    }


def reference(x_prompt, x_sample, state_gla, state_pool, cache_k, cache_v, page_table,
              w_in, w_gate_up, b_gate, gla_norm_g, w_pool, pool_scale, sb_bias, w_out,
              ln1_g, ln1_b, w_ffn_in, w_ffn_out, ln2_g, ln2_b):
    past_len = page_table.shape[1] * PAGE_SIZE
    dec_b, dec_t, _ = x_sample.shape
    xp, xs = x_prompt, x_sample
    gla_p, pool_p, k_p, v_p = [], [], [], []
    gla_s, pool_s, k_s, v_s = [], [], [], []
    for l in range(DEPTH):
        lw = (w_in[l], w_gate_up[l], b_gate[l], gla_norm_g[l], w_pool[l], pool_scale[l], sb_bias[l],
              w_out[l], ln1_g[l], ln1_b[l], w_ffn_in[l], w_ffn_out[l], ln2_g[l], ln2_b[l])
        s0 = jnp.zeros((xp.shape[0], GLA_HEADS, GLA_DK, GLA_DV), xp.dtype)
        b0 = jnp.zeros((xp.shape[0], POOL_BUF, POOL_WIDTH), xp.dtype)
        xp, sg, sb, kk, vv = layer_forward(xp, s0, b0, 0, sb_prompt, *lw)
        gla_p.append(sg); pool_p.append(sb); k_p.append(kk); v_p.append(vv)
        k_past = cache_k[l][page_table].reshape(dec_b, past_len, SB_HEADS, SB_DIM)
        v_past = cache_v[l][page_table].reshape(dec_b, past_len, SB_HEADS, SB_DIM)

        def sb_sample(q, k, v, bias, k_past=k_past, v_past=v_past):
            k_all = jnp.concatenate([k_past, k.astype(k_past.dtype)], axis=1)
            v_all = jnp.concatenate([v_past, v.astype(v_past.dtype)], axis=1)
            k_pos = jnp.arange(past_len + dec_t)
            q_pos = past_len + jnp.arange(dec_t)
            return sb_attend(q, k_all, v_all, bias, q_pos, k_pos)

        xs, sg, sb, kk, vv = layer_forward(xs, state_gla[l], state_pool[l], past_len, sb_sample, *lw)
        gla_s.append(sg); pool_s.append(sb); k_s.append(kk); v_s.append(vv)
    new_gla_prompt = jnp.stack(gla_p)
    new_pool_prompt = jnp.stack(pool_p)
    new_k_prompt = jnp.stack(k_p)
    new_v_prompt = jnp.stack(v_p)
    new_gla_sample = jnp.stack(gla_s)
    new_pool_sample = jnp.stack(pool_s)
    new_k_sample = jnp.stack(k_s)
    new_v_sample = jnp.stack(v_s)
    return (xp, xs, new_gla_prompt, new_pool_prompt, new_k_prompt, new_v_prompt,
            new_gla_sample, new_pool_sample, new_k_sample, new_v_sample)
```

```python
import functools

import jax
import jax.numpy as jnp
from jax import lax
from jax.experimental import pallas as pl
from jax.experimental.pallas import tpu as pltpu

F32 = jnp.float32
BF16 = jnp.bfloat16

GLA_HEADS = 4
GLA_DK = 32
GLA_DV = 64
GLA_KEY = GLA_HEADS * GLA_DK
GLA_VAL = GLA_HEADS * GLA_DV
GATE_RANK = 16
GATE_TAU = 16.0
GLA_CHUNK = 64
POOL_WINDOWS = (2, 4, 8, 16)
POOL_GROUP_DIM = 64
POOL_WIDTH = 256
POOL_BUF = 15
POOL_HALO = 16
SB_HEADS = 8
SB_DIM = 64
SB_WIDTH = SB_HEADS * SB_DIM
SB_TILE = 128
PAGE_SIZE = 128
LN_EPS = 1e-5

LANES = 128
SUBLANES = 8
VMEM_LIMIT = 56 * 1024 * 1024

GLA_IN = GLA_KEY + GLA_KEY + GLA_VAL + GLA_VAL + LANES
C_GLA = 0
C_U = C_GLA + GLA_IN
C_Q = C_U + POOL_WIDTH
C_K = C_Q + SB_WIDTH
C_V = C_K + SB_WIDTH
C_END = C_V + SB_WIDTH


def _cparams(sem):
    return pltpu.CompilerParams(dimension_semantics=sem, vmem_limit_bytes=VMEM_LIMIT)


def _dot(a, b):
    return jnp.dot(a, b, preferred_element_type=F32)


def _dot_nt(a, b):
    return lax.dot_general(a, b, (((1,), (1,)), ((), ())), preferred_element_type=F32)


def _dot_tn(a, b):
    return lax.dot_general(a, b, (((0,), (0,)), ((), ())), preferred_element_type=F32)


def _split3(x):
    h = x.astype(BF16)
    r = x - h.astype(F32)
    m = r.astype(BF16)
    l = (r - m.astype(F32)).astype(BF16)
    return h, m, l


def _split2(x):
    h = x.astype(BF16)
    l = (x - h.astype(F32)).astype(BF16)
    return h, l


def _shr(x, n):
    return lax.shift_right_logical(x, jnp.int32(n))


def _log_sigmoid(z):
    return jnp.minimum(z, 0.0) - jnp.log(1.0 + jnp.exp(-jnp.abs(z)))


def _silu(x):
    return x / (1.0 + jnp.exp(-x))


def _layer_norm(y, g, b):
    mu = jnp.mean(y, axis=-1, keepdims=True)
    d = y - mu
    var = jnp.mean(d * d, axis=-1, keepdims=True)
    return d * lax.rsqrt(var + LN_EPS) * g + b


def _in_proj_kernel(x_ref, w_ref, gla_ref, u_ref, qb_ref, kt_ref, vt_ref, *extra,
                    prompt):
    xb = x_ref[...].astype(BF16)
    nat = lambda lo, hi: _dot_nt(xb, w_ref[lo:hi, :])
    gla_ref[...] = nat(C_GLA, C_U)
    u_ref[...] = nat(C_U, C_Q)
    qb_ref[...] = (nat(C_Q, C_K) * (SB_DIM ** -0.5)).astype(BF16)
    kt = _dot_nt(w_ref[C_K:C_V, :], xb)
    vt = _dot_nt(w_ref[C_V:C_END, :], xb)
    kt_ref[...] = kt
    vt_ref[...] = vt
    if prompt:
        ktb_ref, vtb_ref = extra
        ktb_ref[...] = kt.astype(BF16)
        vtb_ref[...] = vt.astype(BF16)
    else:
        k_ref, v_ref = extra
        k_ref[...] = nat(C_K, C_V)
        v_ref[...] = nat(C_V, C_END)


def _in_proj(x, w_t, layer, *, groups, tm, prompt):
    n, d = x.shape
    glen = n // groups
    per = glen // tm
    row = lambda c: pl.BlockSpec((tm, c), lambda i: (i, 0))
    ft = pl.BlockSpec((None, SB_WIDTH, tm), lambda i: (i // per, 0, i % per))
    ft_shape = lambda dt: jax.ShapeDtypeStruct((groups, SB_WIDTH, glen), dt)
    nat_shape = lambda c, dt: jax.ShapeDtypeStruct((n, c), dt)
    out_specs = [row(GLA_IN), row(POOL_WIDTH), row(SB_WIDTH), ft, ft]
    out_shape = [nat_shape(GLA_IN, F32), nat_shape(POOL_WIDTH, F32),
                 nat_shape(SB_WIDTH, BF16), ft_shape(F32), ft_shape(F32)]
    if prompt:
        out_specs += [ft, ft]
        out_shape += [ft_shape(BF16), ft_shape(BF16)]
    else:
        out_specs += [row(SB_WIDTH), row(SB_WIDTH)]
        out_shape += [nat_shape(SB_WIDTH, F32), nat_shape(SB_WIDTH, F32)]
    return pl.pallas_call(
        functools.partial(_in_proj_kernel, prompt=prompt),
        grid=(n // tm,),
        in_specs=[row(d), pl.BlockSpec((None, C_END, d), lambda i: (layer, 0, 0))],
        out_specs=out_specs,
        out_shape=out_shape,
        compiler_params=_cparams(("arbitrary",)),
        name="in_proj",
    )(x, w_t)


def _gla_kernel(x_ref, s0_ref, wg_ref, bg_ref, gain_ref, tri_ref, hmean_ref,
                o_ref, s_ref, xpad_ref, *, nseq, t_blk, valid):
    L = GLA_CHUNK
    c = pl.program_id(1)

    @pl.when(c == 0)
    def _():
        s_ref[...] = s0_ref[...]

    row = lax.broadcasted_iota(jnp.int32, (L, GLA_KEY), 0)
    lane_k = lax.broadcasted_iota(jnp.int32, (L, GLA_KEY), 1)
    tq = lax.broadcasted_iota(jnp.int32, (GLA_HEADS * L, L), 0)
    ts = lax.broadcasted_iota(jnp.int32, (GLA_HEADS * L, L), 1)
    causal = (tq & (L - 1)) >= ts
    lane_v = lax.broadcasted_iota(jnp.int32, (L, GLA_VAL), 1)
    srow = lax.broadcasted_iota(jnp.int32, (GLA_KEY, GLA_VAL), 0)
    scol = lax.broadcasted_iota(jnp.int32, (GLA_KEY, GLA_VAL), 1)
    sdiag = _shr(srow, 5) == _shr(scol, 6)
    tri = tri_ref[...]
    hmean = hmean_ref[...]

    for i in range(nseq):
        if t_blk < L:
            xpad_ref[...] = jnp.zeros((L, GLA_IN), F32)
            xpad_ref[0:t_blk, :] = x_ref[i]
            blk = xpad_ref[...]
        else:
            blk = x_ref[i]
        q = blk[:, 0:GLA_KEY] * (GLA_DK ** -0.5)
        k = blk[:, GLA_KEY:2 * GLA_KEY]
        v = blk[:, 2 * GLA_KEY:2 * GLA_KEY + GLA_VAL]
        g = blk[:, 2 * GLA_KEY + GLA_VAL:2 * GLA_KEY + 2 * GLA_VAL]
        a = blk[:, 2 * GLA_KEY + 2 * GLA_VAL:GLA_IN]

        z = _dot(a.astype(BF16), wg_ref[...]) + bg_ref[...]
        log_a = _log_sigmoid(z) * (1.0 / GATE_TAU)
        if valid < L:
            log_a = jnp.where(row < valid, log_a, 0.0)
        h3, m3, l3 = _split3(log_a)
        b = _dot(tri, h3) + _dot(tri, m3) + _dot(tri, l3)
        b_last = b[L - 1:L, :]
        b_mid = b[L // 2 - 1:L // 2, :]

        s_bd = s_ref[i]
        vb = v.astype(BF16)
        o_inter = _dot((q * jnp.exp(b)).astype(BF16), s_bd.astype(BF16))
        qm = q * jnp.exp(b - b_mid)
        km = (k * jnp.exp(b_mid - b)).astype(BF16)
        qstack = jnp.concatenate(
            [jnp.where(_shr(lane_k, 5) == h, qm, 0.0) for h in range(GLA_HEADS)],
            axis=0).astype(BF16)
        scores = jnp.where(causal, _dot_nt(qstack, km), 0.0)
        p = _dot(scores.astype(BF16), vb)
        o = o_inter
        for h in range(GLA_HEADS):
            o = o + jnp.where(_shr(lane_v, 6) == h, p[h * L:(h + 1) * L, :], 0.0)

        kd = (k * jnp.exp(b_last - b)).astype(BF16)
        upd = jnp.where(sdiag, _dot_tn(kd, vb), 0.0)
        dcol = jnp.transpose(jnp.broadcast_to(jnp.exp(b_last), (GLA_KEY, GLA_KEY)))
        s_ref[i] = jnp.concatenate([dcol, dcol], axis=1) * s_bd + upd

        oh, om, ol = _split3(o)
        mu = _dot(oh, hmean) + _dot(om, hmean) + _dot(ol, hmean)
        d = o - mu
        dh, dm, dl = _split3(d * d)
        var = _dot(dh, hmean) + _dot(dm, hmean) + _dot(dl, hmean)
        y = d * lax.rsqrt(var + LN_EPS) * gain_ref[...] * _silu(g)
        o_ref[i] = y[0:t_blk, :].astype(BF16)


def _gla(x, s0_bd, wg, layer, bg, gain, *, nseq, valid):
    b, t, _ = x.shape
    L = GLA_CHUNK
    t_blk = L if t >= L else t
    nchunk = t // t_blk
    tri = (lax.broadcasted_iota(jnp.int32, (L, L), 0)
           >= lax.broadcasted_iota(jnp.int32, (L, L), 1)).astype(BF16)
    hm = ((lax.broadcasted_iota(jnp.int32, (GLA_VAL, GLA_VAL), 0) // GLA_DV)
          == (lax.broadcasted_iota(jnp.int32, (GLA_VAL, GLA_VAL), 1) // GLA_DV))
    hmean = jnp.where(hm, 1.0 / GLA_DV, 0.0).astype(BF16)
    const = lambda shape: pl.BlockSpec(shape, lambda s, c: (0,) * len(shape))
    return pl.pallas_call(
        functools.partial(_gla_kernel, nseq=nseq, t_blk=t_blk, valid=valid),
        grid=(b // nseq, nchunk),
        in_specs=[pl.BlockSpec((nseq, t_blk, GLA_IN), lambda s, c: (s, c, 0)),
                  pl.BlockSpec((nseq, GLA_KEY, GLA_VAL), lambda s, c: (s, 0, 0)),
                  pl.BlockSpec((None, LANES, GLA_KEY), lambda s, c: (layer, 0, 0)),
                  const((1, GLA_KEY)), const((1, GLA_VAL)),
                  const((L, L)), const((GLA_VAL, GLA_VAL))],
        out_specs=[pl.BlockSpec((nseq, t_blk, GLA_VAL), lambda s, c: (s, c, 0)),
                   pl.BlockSpec((nseq, GLA_KEY, GLA_VAL), lambda s, c: (s, 0, 0))],
        out_shape=[jax.ShapeDtypeStruct((b, t, GLA_VAL), BF16),
                   jax.ShapeDtypeStruct((b, GLA_KEY, GLA_VAL), F32)],
        scratch_shapes=[pltpu.VMEM((L, GLA_IN), F32)],
        compiler_params=_cparams(("arbitrary", "arbitrary")),
        name="gla",
    )(x, s0_bd, wg, bg, gain, tri, hmean)


def _pool_window(x0, sh, grp):
    s2 = x0 + sh(1)
    s4 = s2 + sh(2) + sh(3)
    s8 = s4 + sh(4) + sh(5) + sh(6) + sh(7)
    s16 = s8 + sh(8) + sh(9) + sh(10) + sh(11) + sh(12) + sh(13) + sh(14) + sh(15)
    return jnp.where(grp == 0, s2, jnp.where(grp == 1, s4, jnp.where(grp == 2, s8, s16)))


def _pool_width(grp):
    return jnp.where(grp == 0, POOL_WINDOWS[0],
                     jnp.where(grp == 1, POOL_WINDOWS[1],
                               jnp.where(grp == 2, POOL_WINDOWS[2], POOL_WINDOWS[3])))


def _pool_prompt_kernel(u_ref, w_ref, scale_ref, o_ref, buf_ref, x_scr, *, tt):
    t = pl.program_id(1)
    H = POOL_HALO

    @pl.when(t == 0)
    def _():
        x_scr[0:H, :] = jnp.zeros((H, POOL_WIDTH), F32)

    x_scr[H:H + tt, :] = u_ref[...]
    x0 = x_scr[H:H + tt, :]
    shape = (tt, POOL_WIDTH)
    grp = _shr(lax.broadcasted_iota(jnp.int32, shape, 1), 6)
    win = _pool_window(x0, lambda j: x_scr[H - j:H - j + tt, :], grp)
    pos = t * tt + lax.broadcasted_iota(jnp.int32, shape, 0)
    cnt = jnp.minimum(pos + 1, _pool_width(grp)).astype(F32)
    d = win / cnt - x0
    o_ref[...] = (_dot(d.astype(BF16), w_ref[...]) * scale_ref[...]).astype(BF16)

    @pl.when(t == pl.num_programs(1) - 1)
    def _():
        buf_ref[...] = x_scr[H + tt - POOL_BUF:H + tt, :]

    x_scr[0:H, :] = x_scr[tt:tt + H, :]


def _pool_prompt(u, w_bd, layer, scale, *, tt):
    b, t, w = u.shape
    return pl.pallas_call(
        functools.partial(_pool_prompt_kernel, tt=tt),
        grid=(b, t // tt),
        in_specs=[pl.BlockSpec((None, tt, w), lambda s, i: (s, i, 0)),
                  pl.BlockSpec((None, w, w), lambda s, i: (layer, 0, 0)),
                  pl.BlockSpec((1, w), lambda s, i: (0, 0))],
        out_specs=[pl.BlockSpec((None, tt, w), lambda s, i: (s, i, 0)),
                   pl.BlockSpec((None, POOL_BUF, w), lambda s, i: (s, 0, 0))],
        out_shape=[jax.ShapeDtypeStruct((b, t, w), BF16),
                   jax.ShapeDtypeStruct((b, POOL_BUF, w), F32)],
        scratch_shapes=[pltpu.VMEM((POOL_HALO + tt, w), F32)],
        compiler_params=_cparams(("arbitrary", "arbitrary")),
        name="pool_prompt",
    )(u, w_bd, scale)


def _pool_sample_kernel(u_ref, prev_ref, w_ref, scale_ref, o_ref, buf_ref, *, nt, pos0):
    ext = [prev_ref[i] for i in range(POOL_BUF)] + [u_ref[t] for t in range(nt)]
    shape = ext[0].shape
    grp = _shr(lax.broadcasted_iota(jnp.int32, shape, 1), 6)
    width = _pool_width(grp)
    for t in range(nt):
        x0 = ext[POOL_BUF + t]
        win = _pool_window(x0, lambda j: ext[POOL_BUF + t - j], grp)
        cnt = jnp.minimum(pos0 + t + 1, width).astype(F32)
        d = win / cnt - x0
        o_ref[t] = (_dot(d.astype(BF16), w_ref[...]) * scale_ref[...]).astype(BF16)
    for i in range(POOL_BUF):
        buf_ref[i] = ext[nt + i]


def _pool_sample(u, prev_all, w_bd, layer, scale, *, pos0):
    nt, b, w = u.shape
    return pl.pallas_call(
        functools.partial(_pool_sample_kernel, nt=nt, pos0=pos0),
        grid=(1,),
        in_specs=[pl.BlockSpec((nt, b, w), lambda i: (0, 0, 0)),
                  pl.BlockSpec((None, POOL_BUF, b, w), lambda i: (layer, 0, 0, 0)),
                  pl.BlockSpec((None, w, w), lambda i: (layer, 0, 0)),
                  pl.BlockSpec((1, w), lambda i: (0, 0))],
        out_specs=[pl.BlockSpec((nt, b, w), lambda i: (0, 0, 0)),
                   pl.BlockSpec((POOL_BUF, b, w), lambda i: (0, 0, 0))],
        out_shape=[jax.ShapeDtypeStruct((nt, b, w), BF16),
                   jax.ShapeDtypeStruct((POOL_BUF, b, w), F32)],
        compiler_params=_cparams(("arbitrary",)),
        name="pool_sample",
    )(u, prev_all, w_bd, scale)


def _sb_tile(qh, kt, vt, bias, uj, carry, acc, mask, feature_major):
    z = (_dot(qh, kt) if feature_major else _dot_nt(qh, kt)) + bias
    sp = jnp.log(1.0 + jnp.exp(-jnp.abs(z)))
    log_beta = jnp.minimum(z, 0.0) - sp
    log_1m = log_beta - z
    if mask is not None:
        log_1m = jnp.where(mask, log_1m, 0.0)
    hi, lo = _split2(log_1m)
    er = _dot(hi, uj) + _dot(lo, uj)
    a = jnp.exp(log_beta + er[:, :SB_TILE] + carry)
    if mask is not None:
        a = jnp.where(mask, a, 0.0)
    ab = a.astype(BF16)
    acc = acc + (_dot_nt(ab, vt) if feature_major else _dot(ab, vt))
    carry = carry + er[:, SB_TILE:]
    return carry, acc


def _sb_uj():
    r = lax.broadcasted_iota(jnp.int32, (SB_TILE, 2 * SB_TILE), 0)
    c = lax.broadcasted_iota(jnp.int32, (SB_TILE, 2 * SB_TILE), 1)
    return ((c >= SB_TILE) | (r > c)).astype(BF16)


def _sb_prompt_kernel(bias_ref, q_ref, k_ref, v_ref, uj_ref, o_ref):
    pair = pl.program_id(1)
    qi = pl.program_id(2)
    T = SB_TILE
    q = q_ref[...]
    lane = lax.broadcasted_iota(jnp.int32, (T, T), 1)
    row = lax.broadcasted_iota(jnp.int32, (T, T), 0)
    diag = lane < row
    uj = uj_ref[...]
    zero = jnp.zeros((T, T), F32)
    accs = []
    for h in range(2):
        head_lanes = (lane < SB_DIM) if h == 0 else (lane >= SB_DIM)
        qh = jnp.where(head_lanes, q, jnp.zeros_like(q))
        bias = bias_ref[2 * pair + h]

        def tile(j, carry, acc, mask):
            start = pl.multiple_of(j * T, T)
            return _sb_tile(qh, k_ref[:, pl.ds(start, T)], v_ref[:, pl.ds(start, T)],
                            bias, uj, carry, acc, mask, True)

        carry, acc = tile(qi, zero, zero, diag)
        carry, acc = lax.fori_loop(
            0, qi, lambda s, ca: tile(qi - 1 - s, ca[0], ca[1], None), (carry, acc))
        accs.append(acc)
    o_ref[...] = jnp.where(lane < SB_DIM, accs[0], accs[1]).astype(BF16)


def _sb_prompt(qb, ktb, vtb, bias, seq):
    T = SB_TILE
    batch = ktb.shape[0]
    nq = seq // T
    return pl.pallas_call(
        _sb_prompt_kernel,
        grid=(batch, SB_HEADS // 2, nq),
        in_specs=[pl.BlockSpec(memory_space=pltpu.SMEM),
                  pl.BlockSpec((T, LANES), lambda b, p, i: (b * nq + i, p)),
                  pl.BlockSpec((None, LANES, seq), lambda b, p, i: (b, p, 0)),
                  pl.BlockSpec((None, LANES, seq), lambda b, p, i: (b, p, 0)),
                  pl.BlockSpec((T, 2 * T), lambda b, p, i: (0, 0))],
        out_specs=pl.BlockSpec((T, LANES), lambda b, p, i: (b * nq + i, p)),
        out_shape=jax.ShapeDtypeStruct((batch * seq, SB_WIDTH), BF16),
        compiler_params=_cparams(("arbitrary", "arbitrary", "arbitrary")),
        name="sb_prompt",
    )(bias, qb, ktb, vtb, _sb_uj())


def _sb_sample_kernel(pt_ref, q_ref, kn_ref, vn_ref, bias_ref, uj_ref, *rest, npage, tq):
    k_refs = rest[:npage]
    v_refs = rest[npage:2 * npage]
    o_ref = rest[2 * npage]
    acc_ref, carry_ref, pad_ref = rest[2 * npage + 1:]
    j = pl.program_id(1)
    T = SB_TILE
    R = SB_HEADS * tq
    q = q_ref[0]
    lane = lax.broadcasted_iota(jnp.int32, (tq, SB_WIDTH), 1)
    qbd = jnp.concatenate(
        [jnp.where(_shr(lane, 6) == h, q, 0.0) for h in range(SB_HEADS)],
        axis=0).astype(BF16)
    bias = bias_ref[...]
    uj = uj_ref[...]

    def padded(ref):
        pad_ref[...] = jnp.zeros((T, SB_WIDTH), F32)
        pad_ref[0:tq, :] = ref[0]
        return pad_ref[...].astype(BF16)

    @pl.when(j == 0)
    def _():
        kn = padded(kn_ref)
        vn = padded(vn_ref)
        key = lax.broadcasted_iota(jnp.int32, (R, T), 1)
        qt = lax.broadcasted_iota(jnp.int32, (R, T), 0) & (tq - 1)
        carry, acc = _sb_tile(qbd, kn, vn, bias, uj, jnp.zeros((R, T), F32),
                              jnp.zeros((R, SB_WIDTH), F32), key < qt, False)
        carry_ref[...] = carry
        acc_ref[...] = acc

    carry = carry_ref[...]
    acc = acc_ref[...]
    for i in range(npage):
        carry, acc = _sb_tile(qbd, k_refs[i][...].astype(BF16), v_refs[i][...].astype(BF16),
                              bias, uj, carry, acc, None, True)
    carry_ref[...] = carry
    acc_ref[...] = acc

    @pl.when(j == pl.num_programs(1) - 1)
    def _():
        out = jnp.zeros((tq, SB_WIDTH), F32)
        for h in range(SB_HEADS):
            out = out + jnp.where(_shr(lane, 6) == h, acc[h * tq:(h + 1) * tq, :], 0.0)
        o_ref[0] = out.astype(BF16)


def _sb_sample(q8, kn8, vn8, bias, cache_kt, cache_vt, page_table, layer, *, npage):
    b, tq, _ = q8.shape
    n_pages = page_table.shape[1]
    R = SB_HEADS * tq
    bias_tile = jnp.broadcast_to(jnp.repeat(bias.astype(F32), tq)[:, None], (R, SB_TILE))
    tok = pl.BlockSpec((1, tq, SB_WIDTH), lambda s, j, pt: (s, 0, 0))

    def page_spec(i):
        return pl.BlockSpec(
            (None, None, SB_WIDTH, PAGE_SIZE),
            lambda s, j, pt: (layer, pt[s, n_pages - 1 - (j * npage + i)], 0, 0))

    grid_spec = pltpu.PrefetchScalarGridSpec(
        num_scalar_prefetch=1,
        grid=(b, n_pages // npage),
        in_specs=[tok, tok, tok,
                  pl.BlockSpec((R, SB_TILE), lambda s, j, pt: (0, 0)),
                  pl.BlockSpec((SB_TILE, 2 * SB_TILE), lambda s, j, pt: (0, 0))]
                 + [page_spec(i) for i in range(npage)]
                 + [page_spec(i) for i in range(npage)],
        out_specs=pl.BlockSpec((1, tq, SB_WIDTH), lambda s, j, pt: (s, 0, 0)),
        scratch_shapes=[pltpu.VMEM((R, SB_WIDTH), F32), pltpu.VMEM((R, SB_TILE), F32),
                        pltpu.VMEM((SB_TILE, SB_WIDTH), F32)],
    )
    return pl.pallas_call(
        functools.partial(_sb_sample_kernel, npage=npage, tq=tq),
        grid_spec=grid_spec,
        out_shape=jax.ShapeDtypeStruct((b, tq, SB_WIDTH), BF16),
        compiler_params=_cparams(("arbitrary", "arbitrary")),
        name="sb_sample",
    )(page_table, q8, kn8, vn8, bias_tile, _sb_uj(),
      *([cache_kt] * npage), *([cache_vt] * npage))


def _out_proj_kernel(og_ref, op_ref, os_ref, x_ref, w_ref, g_ref, b_ref, o_ref, *, alpha):
    mix = (_dot(og_ref[...], w_ref[0:GLA_VAL, :])
           + _dot(op_ref[...], w_ref[GLA_VAL:GLA_VAL + POOL_WIDTH, :])
           + _dot(os_ref[...], w_ref[GLA_VAL + POOL_WIDTH:, :]))
    o_ref[...] = _layer_norm(alpha * x_ref[...] + mix, g_ref[...], b_ref[...])


def _out_proj(og, op, os_, x, w, layer, g, b, alpha):
    n, d = x.shape
    tm = min(n, 512)
    row = lambda c: pl.BlockSpec((tm, c), lambda i: (i, 0))
    const = lambda r, c: pl.BlockSpec((r, c), lambda i: (0, 0))
    return pl.pallas_call(
        functools.partial(_out_proj_kernel, alpha=alpha),
        grid=(n // tm,),
        in_specs=[row(GLA_VAL), row(POOL_WIDTH), row(SB_WIDTH), row(d),
                  pl.BlockSpec((None, w.shape[1], d), lambda i: (layer, 0, 0)),
                  const(1, d), const(1, d)],
        out_specs=row(d),
        out_shape=jax.ShapeDtypeStruct((n, d), F32),
        compiler_params=_cparams(("arbitrary",)),
        name="out_proj_ln",
    )(og, op, os_, x, w, g, b)


def _ffn_kernel(x_ref, wg_ref, wu_ref, wo_ref, g_ref, b_ref, o_ref, acc_ref, xb_ref, *, alpha):
    f = pl.program_id(1)

    @pl.when(f == 0)
    def _():
        acc_ref[...] = jnp.zeros_like(acc_ref)
        xb_ref[...] = x_ref[...].astype(BF16)

    xb = xb_ref[...]
    gate = _dot(xb, wg_ref[...])
    up = _dot(xb, wu_ref[...])
    acc_ref[...] += _dot((_silu(gate) * up).astype(BF16), wo_ref[...])

    @pl.when(f == pl.num_programs(1) - 1)
    def _():
        o_ref[...] = _layer_norm(alpha * x_ref[...] + acc_ref[...], g_ref[...], b_ref[...])


def _ffn(x, w_in, w_out, layer, g, b, alpha, *, tm, tf):
    n, d = x.shape
    hidden = w_out.shape[1]
    nf = hidden // tf
    return pl.pallas_call(
        functools.partial(_ffn_kernel, alpha=alpha),
        grid=(n // tm, nf),
        in_specs=[pl.BlockSpec((tm, d), lambda i, f: (i, 0)),
                  pl.BlockSpec((None, d, tf), lambda i, f: (layer, 0, f)),
                  pl.BlockSpec((None, d, tf), lambda i, f: (layer, 0, nf + f)),
                  pl.BlockSpec((None, tf, d), lambda i, f: (layer, f, 0)),
                  pl.BlockSpec((1, d), lambda i, f: (0, 0)),
                  pl.BlockSpec((1, d), lambda i, f: (0, 0))],
        out_specs=pl.BlockSpec((tm, d), lambda i, f: (i, 0)),
        out_shape=jax.ShapeDtypeStruct((n, d), F32),
        scratch_shapes=[pltpu.VMEM((tm, d), F32), pltpu.VMEM((tm, d), BF16)],
        compiler_params=_cparams(("arbitrary", "arbitrary")),
        name="ffn_ln",
    )(x, w_in, w_in, w_out, g, b)


def _gla_state_to_bd(s):
    b = s.shape[0]
    eye = jnp.eye(GLA_HEADS, dtype=bool)[None, :, None, :, None]
    bd = jnp.where(eye, s[:, :, :, None, :], 0.0)
    return bd.reshape(b, GLA_KEY, GLA_VAL)


def _gla_state_from_bd(bd):
    b = bd.shape[0]
    s = bd.reshape(b, GLA_HEADS, GLA_DK, GLA_HEADS, GLA_DV)
    return jnp.stack([s[:, h, :, h, :] for h in range(GLA_HEADS)], axis=1)


def kernel(x_prompt, x_sample, state_gla, state_pool, cache_k, cache_v, page_table,
           w_in, w_gate_up, b_gate, gla_norm_g, w_pool, pool_scale, sb_bias, w_out,
           ln1_g, ln1_b, w_ffn_in, w_ffn_out, ln2_g, ln2_b):
    depth = w_in.shape[0]
    batch, seq, d_model = x_prompt.shape
    dec_b, dec_t, _ = x_sample.shape
    n_phys = cache_k.shape[1]
    past_len = page_table.shape[1] * PAGE_SIZE
    alpha = (2.0 * depth) ** 0.25
    assert cache_k.shape[2] == PAGE_SIZE and seq % 512 == 0
    assert dec_t <= SUBLANES and dec_b % LANES == 0
    t_pad = SUBLANES

    w_in_t = jnp.transpose(w_in, (0, 2, 1))
    o = 0
    rows = {}
    for name, size in (("q", GLA_KEY), ("k", GLA_KEY), ("v", GLA_VAL), ("a", GATE_RANK),
                       ("g", GLA_VAL), ("u", POOL_WIDTH), ("qs", SB_WIDTH),
                       ("ks", SB_WIDTH), ("vs", SB_WIDTH)):
        rows[name] = w_in_t[:, o:o + size, :]
        o += size
    a_pad = jnp.zeros((depth, LANES - GATE_RANK, d_model), w_in.dtype)
    w_in_p = jnp.concatenate(
        [rows["q"], rows["k"], rows["v"], rows["g"], rows["a"], a_pad,
         rows["u"], rows["qs"], rows["ks"], rows["vs"]], axis=1).astype(BF16)
    wg_p = jnp.pad(w_gate_up, ((0, 0), (0, LANES - GATE_RANK), (0, 0))).astype(BF16)
    ngrp = len(POOL_WINDOWS)
    eye_g = jnp.eye(ngrp, dtype=bool)[None, :, None, :, None]
    w_pool_bd = jnp.where(eye_g, w_pool[:, :, :, None, :], 0.0).reshape(
        depth, POOL_WIDTH, POOL_WIDTH).astype(BF16)
    w_out_b = w_out.astype(BF16)
    w_ffn_in_b = w_ffn_in.astype(BF16)
    w_ffn_out_b = w_ffn_out.astype(BF16)
    cache_kt = jnp.transpose(cache_k, (0, 1, 3, 4, 2)).reshape(depth, n_phys, SB_WIDTH, PAGE_SIZE)
    cache_vt = jnp.transpose(cache_v, (0, 1, 3, 4, 2)).reshape(depth, n_phys, SB_WIDTH, PAGE_SIZE)
    pool_prev = jnp.transpose(state_pool, (0, 2, 1, 3))
    row2 = lambda a: a.reshape(1, -1).astype(F32)

    xp = x_prompt.reshape(batch * seq, d_model)
    xs = jnp.transpose(x_sample, (1, 0, 2)).reshape(dec_t * dec_b, d_model)
    zeros_state = jnp.zeros((batch, GLA_KEY, GLA_VAL), F32)
    outs = {k: [] for k in ("gla_p", "pool_p", "k_p", "v_p", "gla_s", "pool_s", "k_s", "v_s")}

    def seq_major(a):
        a = jnp.transpose(a.reshape(dec_t, dec_b, a.shape[-1]), (1, 0, 2))
        return jnp.pad(a, ((0, 0), (0, t_pad - dec_t), (0, 0)))

    def tok_major(a):
        return jnp.transpose(a[:, :dec_t, :], (1, 0, 2)).reshape(dec_t * dec_b, a.shape[-1])

    for l in range(depth):
        bg, gain = row2(b_gate[l]), row2(gla_norm_g[l])
        pscale = row2(pool_scale[l])
        g1, b1, g2, b2 = row2(ln1_g[l]), row2(ln1_b[l]), row2(ln2_g[l]), row2(ln2_b[l])

        gla_in, u, qb, kt, vt, ktb, vtb = _in_proj(xp, w_in_p, l, groups=batch, tm=512,
                                                   prompt=True)
        o_gla, s_bd = _gla(gla_in.reshape(batch, seq, GLA_IN), zeros_state, wg_p, l, bg, gain,
                           nseq=batch, valid=GLA_CHUNK)
        o_pool, pbuf = _pool_prompt(u.reshape(batch, seq, POOL_WIDTH), w_pool_bd, l, pscale,
                                    tt=512)
        o_sb = _sb_prompt(qb, ktb, vtb, sb_bias[l].astype(F32), seq)
        xp = _out_proj(o_gla.reshape(batch * seq, GLA_VAL), o_pool.reshape(batch * seq, POOL_WIDTH),
                       o_sb, xp, w_out_b, l, g1, b1, alpha)
        xp = _ffn(xp, w_ffn_in_b, w_ffn_out_b, l, g2, b2, alpha, tm=1024, tf=256)
        outs["gla_p"].append(_gla_state_from_bd(s_bd))
        outs["pool_p"].append(pbuf)
        outs["k_p"].append(kt)
        outs["v_p"].append(vt)

        gla_in, u, qb, kt, vt, k, v = _in_proj(xs, w_in_p, l, groups=dec_t, tm=dec_b,
                                               prompt=False)
        o_gla, s_bd = _gla(seq_major(gla_in), _gla_state_to_bd(state_gla[l]), wg_p, l, bg, gain,
                           nseq=8, valid=dec_t)
        o_pool, pbuf = _pool_sample(u.reshape(dec_t, dec_b, POOL_WIDTH), pool_prev, w_pool_bd, l,
                                    pscale, pos0=past_len)
        o_sb = _sb_sample(seq_major(qb.astype(F32)), seq_major(k), seq_major(v), sb_bias[l],
                          cache_kt, cache_vt, page_table, l, npage=4)
        xs = _out_proj(tok_major(o_gla), o_pool.reshape(dec_t * dec_b, POOL_WIDTH),
                       tok_major(o_sb), xs, w_out_b, l, g1, b1, alpha)
        xs = _ffn(xs, w_ffn_in_b, w_ffn_out_b, l, g2, b2, alpha, tm=512, tf=256)
        outs["gla_s"].append(_gla_state_from_bd(s_bd))
        outs["pool_s"].append(pbuf)
        outs["k_s"].append(kt)
        outs["v_s"].append(vt)

    st = lambda key: jnp.stack(outs[key])
    kv_p = lambda key: jnp.transpose(
        st(key).reshape(depth, batch, SB_HEADS, SB_DIM, seq), (0, 1, 4, 2, 3))
    kv_s = lambda key: jnp.transpose(
        st(key).reshape(depth, dec_t, SB_HEADS, SB_DIM, dec_b), (0, 4, 1, 2, 3))
    y_sample = jnp.transpose(xs.reshape(dec_t, dec_b, d_model), (1, 0, 2))
    return (xp.reshape(batch, seq, d_model), y_sample,
            st("gla_p"), st("pool_p"), kv_p("k_p"), kv_p("v_p"),
            st("gla_s"), jnp.transpose(st("pool_s"), (0, 2, 1, 3)), kv_s("k_s"), kv_s("v_s"))
```

```python
import functools

import jax
import jax.numpy as jnp
from jax import lax
from jax.experimental import pallas as pl
from jax.experimental.pallas import tpu as pltpu

F32 = jnp.float32
BF16 = jnp.bfloat16

GLA_HEADS = 4
GLA_DK = 32
GLA_DV = 64
GLA_KEY = GLA_HEADS * GLA_DK
GLA_VAL = GLA_HEADS * GLA_DV
GATE_RANK = 16
GATE_TAU = 16.0
GLA_CHUNK = 64
POOL_WINDOWS = (2, 4, 8, 16)
POOL_GROUP_DIM = 64
POOL_WIDTH = 256
POOL_BUF = 15
POOL_HALO = 16
SB_HEADS = 8
SB_DIM = 64
SB_WIDTH = SB_HEADS * SB_DIM
SB_TILE = 128
SB_QBLOCK = 4 * SB_TILE
SB_MASKED = -1e30
PAGE_SIZE = 128
LN_EPS = 1e-5

LANES = 128
SUBLANES = 8
VMEM_LIMIT = 56 * 1024 * 1024

GLA_IN = GLA_KEY + GLA_KEY + GLA_VAL + GLA_VAL + LANES
C_GLA = 0
C_U = C_GLA + GLA_IN
C_Q = C_U + POOL_WIDTH
C_K = C_Q + SB_WIDTH
C_V = C_K + SB_WIDTH
C_END = C_V + SB_WIDTH


def _cparams(sem):
    return pltpu.CompilerParams(dimension_semantics=sem, vmem_limit_bytes=VMEM_LIMIT)


def _dot(a, b):
    return jnp.dot(a, b, preferred_element_type=F32)


def _dot_nt(a, b):
    return lax.dot_general(a, b, (((1,), (1,)), ((), ())), preferred_element_type=F32)


def _dot_tn(a, b):
    return lax.dot_general(a, b, (((0,), (0,)), ((), ())), preferred_element_type=F32)


def _split3(x):
    h = x.astype(BF16)
    r = x - h.astype(F32)
    m = r.astype(BF16)
    l = (r - m.astype(F32)).astype(BF16)
    return h, m, l


def _split2(x):
    h = x.astype(BF16)
    l = (x - h.astype(F32)).astype(BF16)
    return h, l


def _shr(x, n):
    return lax.shift_right_logical(x, jnp.int32(n))


def _log_sigmoid(z):
    return jnp.minimum(z, 0.0) - jnp.log(1.0 + jnp.exp(-jnp.abs(z)))


def _silu(x):
    return x / (1.0 + jnp.exp(-x))


def _layer_norm(y, g, b):
    mu = jnp.mean(y, axis=-1, keepdims=True)
    d = y - mu
    var = jnp.mean(d * d, axis=-1, keepdims=True)
    return d * lax.rsqrt(var + LN_EPS) * g + b


def _in_proj_kernel(x_ref, w_ref, gla_ref, u_ref, qb_ref, kt_ref, vt_ref, *extra,
                    prompt):
    xb = x_ref[...].astype(BF16)
    nat = lambda lo, hi: _dot_nt(xb, w_ref[lo:hi, :])
    gla_ref[...] = nat(C_GLA, C_U)
    u_ref[...] = nat(C_U, C_Q)
    qb_ref[...] = (nat(C_Q, C_K) * (SB_DIM ** -0.5)).astype(BF16)
    kt = _dot_nt(w_ref[C_K:C_V, :], xb)
    vt = _dot_nt(w_ref[C_V:C_END, :], xb)
    kt_ref[...] = kt
    vt_ref[...] = vt
    if prompt:
        ktb_ref, vtb_ref = extra
        ktb_ref[...] = kt.astype(BF16)
        vtb_ref[...] = vt.astype(BF16)
    else:
        k_ref, v_ref = extra
        k_ref[...] = nat(C_K, C_V)
        v_ref[...] = nat(C_V, C_END)


def _in_proj(x, w_t, layer, *, groups, tm, prompt):
    n, d = x.shape
    glen = n // groups
    per = glen // tm
    row = lambda c: pl.BlockSpec((tm, c), lambda i: (i, 0))
    ft = pl.BlockSpec((None, SB_WIDTH, tm), lambda i: (i // per, 0, i % per))
    ft_shape = lambda dt: jax.ShapeDtypeStruct((groups, SB_WIDTH, glen), dt)
    nat_shape = lambda c, dt: jax.ShapeDtypeStruct((n, c), dt)
    out_specs = [row(GLA_IN), row(POOL_WIDTH), row(SB_WIDTH), ft, ft]
    out_shape = [nat_shape(GLA_IN, F32), nat_shape(POOL_WIDTH, F32),
                 nat_shape(SB_WIDTH, BF16), ft_shape(F32), ft_shape(F32)]
    if prompt:
        out_specs += [ft, ft]
        out_shape += [ft_shape(BF16), ft_shape(BF16)]
    else:
        out_specs += [row(SB_WIDTH), row(SB_WIDTH)]
        out_shape += [nat_shape(SB_WIDTH, F32), nat_shape(SB_WIDTH, F32)]
    return pl.pallas_call(
        functools.partial(_in_proj_kernel, prompt=prompt),
        grid=(n // tm,),
        in_specs=[row(d), pl.BlockSpec((None, C_END, d), lambda i: (layer, 0, 0))],
        out_specs=out_specs,
        out_shape=out_shape,
        compiler_params=_cparams(("arbitrary",)),
        name="in_proj",
    )(x, w_t)


def _gla_kernel(x_ref, s0_ref, wg_ref, bg_ref, gain_ref, tri_ref, hmean_ref,
                o_ref, s_ref, xpad_ref, *, nseq, t_blk, valid):
    L = GLA_CHUNK
    c = pl.program_id(1)

    @pl.when(c == 0)
    def _():
        s_ref[...] = s0_ref[...]

    row = lax.broadcasted_iota(jnp.int32, (L, GLA_KEY), 0)
    lane_k = lax.broadcasted_iota(jnp.int32, (L, GLA_KEY), 1)
    tq = lax.broadcasted_iota(jnp.int32, (GLA_HEADS * L, L), 0)
    ts = lax.broadcasted_iota(jnp.int32, (GLA_HEADS * L, L), 1)
    causal = (tq & (L - 1)) >= ts
    lane_v = lax.broadcasted_iota(jnp.int32, (L, GLA_VAL), 1)
    srow = lax.broadcasted_iota(jnp.int32, (GLA_KEY, GLA_VAL), 0)
    scol = lax.broadcasted_iota(jnp.int32, (GLA_KEY, GLA_VAL), 1)
    sdiag = _shr(srow, 5) == _shr(scol, 6)
    tri = tri_ref[...]
    hmean = hmean_ref[...]

    for i in range(nseq):
        if t_blk < L:
            xpad_ref[...] = jnp.zeros((L, GLA_IN), F32)
            xpad_ref[0:t_blk, :] = x_ref[i]
            blk = xpad_ref[...]
        else:
            blk = x_ref[i]
        q = blk[:, 0:GLA_KEY] * (GLA_DK ** -0.5)
        k = blk[:, GLA_KEY:2 * GLA_KEY]
        v = blk[:, 2 * GLA_KEY:2 * GLA_KEY + GLA_VAL]
        g = blk[:, 2 * GLA_KEY + GLA_VAL:2 * GLA_KEY + 2 * GLA_VAL]
        a = blk[:, 2 * GLA_KEY + 2 * GLA_VAL:GLA_IN]

        z = _dot(a.astype(BF16), wg_ref[...]) + bg_ref[...]
        log_a = _log_sigmoid(z) * (1.0 / GATE_TAU)
        if valid < L:
            log_a = jnp.where(row < valid, log_a, 0.0)
        h3, m3, l3 = _split3(log_a)
        b = _dot(tri, h3) + _dot(tri, m3) + _dot(tri, l3)
        b_last = b[L - 1:L, :]
        b_mid = b[L // 2 - 1:L // 2, :]

        s_bd = s_ref[i]
        vb = v.astype(BF16)
        o_inter = _dot((q * jnp.exp(b)).astype(BF16), s_bd.astype(BF16))
        qm = q * jnp.exp(b - b_mid)
        km = (k * jnp.exp(b_mid - b)).astype(BF16)
        qstack = jnp.concatenate(
            [jnp.where(_shr(lane_k, 5) == h, qm, 0.0) for h in range(GLA_HEADS)],
            axis=0).astype(BF16)
        scores = jnp.where(causal, _dot_nt(qstack, km), 0.0)
        p = _dot(scores.astype(BF16), vb)
        o = o_inter
        for h in range(GLA_HEADS):
            o = o + jnp.where(_shr(lane_v, 6) == h, p[h * L:(h + 1) * L, :], 0.0)

        kd = (k * jnp.exp(b_last - b)).astype(BF16)
        upd = jnp.where(sdiag, _dot_tn(kd, vb), 0.0)
        dcol = jnp.transpose(jnp.broadcast_to(jnp.exp(b_last), (GLA_KEY, GLA_KEY)))
        s_ref[i] = jnp.concatenate([dcol, dcol], axis=1) * s_bd + upd

        oh, om, ol = _split3(o)
        mu = _dot(oh, hmean) + _dot(om, hmean) + _dot(ol, hmean)
        d = o - mu
        dh, dm, dl = _split3(d * d)
        var = _dot(dh, hmean) + _dot(dm, hmean) + _dot(dl, hmean)
        y = d * lax.rsqrt(var + LN_EPS) * gain_ref[...] * _silu(g)
        o_ref[i] = y[0:t_blk, :].astype(BF16)


def _gla(x, s0_bd, wg, layer, bg, gain, *, nseq, valid):
    b, t, _ = x.shape
    L = GLA_CHUNK
    t_blk = L if t >= L else t
    nchunk = t // t_blk
    tri = (lax.broadcasted_iota(jnp.int32, (L, L), 0)
           >= lax.broadcasted_iota(jnp.int32, (L, L), 1)).astype(BF16)
    hm = ((lax.broadcasted_iota(jnp.int32, (GLA_VAL, GLA_VAL), 0) // GLA_DV)
          == (lax.broadcasted_iota(jnp.int32, (GLA_VAL, GLA_VAL), 1) // GLA_DV))
    hmean = jnp.where(hm, 1.0 / GLA_DV, 0.0).astype(BF16)
    const = lambda shape: pl.BlockSpec(shape, lambda s, c: (0,) * len(shape))
    return pl.pallas_call(
        functools.partial(_gla_kernel, nseq=nseq, t_blk=t_blk, valid=valid),
        grid=(b // nseq, nchunk),
        in_specs=[pl.BlockSpec((nseq, t_blk, GLA_IN), lambda s, c: (s, c, 0)),
                  pl.BlockSpec((nseq, GLA_KEY, GLA_VAL), lambda s, c: (s, 0, 0)),
                  pl.BlockSpec((None, LANES, GLA_KEY), lambda s, c: (layer, 0, 0)),
                  const((1, GLA_KEY)), const((1, GLA_VAL)),
                  const((L, L)), const((GLA_VAL, GLA_VAL))],
        out_specs=[pl.BlockSpec((nseq, t_blk, GLA_VAL), lambda s, c: (s, c, 0)),
                   pl.BlockSpec((nseq, GLA_KEY, GLA_VAL), lambda s, c: (s, 0, 0))],
        out_shape=[jax.ShapeDtypeStruct((b, t, GLA_VAL), BF16),
                   jax.ShapeDtypeStruct((b, GLA_KEY, GLA_VAL), F32)],
        scratch_shapes=[pltpu.VMEM((L, GLA_IN), F32)],
        compiler_params=_cparams(("arbitrary", "arbitrary")),
        name="gla",
    )(x, s0_bd, wg, bg, gain, tri, hmean)


def _pool_window(x0, sh, grp):
    s2 = x0 + sh(1)
    s4 = s2 + sh(2) + sh(3)
    s8 = s4 + sh(4) + sh(5) + sh(6) + sh(7)
    s16 = s8 + sh(8) + sh(9) + sh(10) + sh(11) + sh(12) + sh(13) + sh(14) + sh(15)
    return jnp.where(grp == 0, s2, jnp.where(grp == 1, s4, jnp.where(grp == 2, s8, s16)))


def _pool_width(grp):
    return jnp.where(grp == 0, POOL_WINDOWS[0],
                     jnp.where(grp == 1, POOL_WINDOWS[1],
                               jnp.where(grp == 2, POOL_WINDOWS[2], POOL_WINDOWS[3])))


def _pool_prompt_kernel(u_ref, w_ref, scale_ref, o_ref, buf_ref, x_scr, *, tt):
    t = pl.program_id(1)
    H = POOL_HALO

    @pl.when(t == 0)
    def _():
        x_scr[0:H, :] = jnp.zeros((H, POOL_WIDTH), F32)

    x_scr[H:H + tt, :] = u_ref[...]
    x0 = x_scr[H:H + tt, :]
    shape = (tt, POOL_WIDTH)
    grp = _shr(lax.broadcasted_iota(jnp.int32, shape, 1), 6)
    win = _pool_window(x0, lambda j: x_scr[H - j:H - j + tt, :], grp)
    pos = t * tt + lax.broadcasted_iota(jnp.int32, shape, 0)
    cnt = jnp.minimum(pos + 1, _pool_width(grp)).astype(F32)
    d = win / cnt - x0
    o_ref[...] = (_dot(d.astype(BF16), w_ref[...]) * scale_ref[...]).astype(BF16)

    @pl.when(t == pl.num_programs(1) - 1)
    def _():
        buf_ref[...] = x_scr[H + tt - POOL_BUF:H + tt, :]

    x_scr[0:H, :] = x_scr[tt:tt + H, :]


def _pool_prompt(u, w_bd, layer, scale, *, tt):
    b, t, w = u.shape
    return pl.pallas_call(
        functools.partial(_pool_prompt_kernel, tt=tt),
        grid=(b, t // tt),
        in_specs=[pl.BlockSpec((None, tt, w), lambda s, i: (s, i, 0)),
                  pl.BlockSpec((None, w, w), lambda s, i: (layer, 0, 0)),
                  pl.BlockSpec((1, w), lambda s, i: (0, 0))],
        out_specs=[pl.BlockSpec((None, tt, w), lambda s, i: (s, i, 0)),
                   pl.BlockSpec((None, POOL_BUF, w), lambda s, i: (s, 0, 0))],
        out_shape=[jax.ShapeDtypeStruct((b, t, w), BF16),
                   jax.ShapeDtypeStruct((b, POOL_BUF, w), F32)],
        scratch_shapes=[pltpu.VMEM((POOL_HALO + tt, w), F32)],
        compiler_params=_cparams(("arbitrary", "arbitrary")),
        name="pool_prompt",
    )(u, w_bd, scale)


def _pool_sample_kernel(u_ref, prev_ref, w_ref, scale_ref, o_ref, buf_ref, *, nt, pos0):
    ext = [prev_ref[i] for i in range(POOL_BUF)] + [u_ref[t] for t in range(nt)]
    shape = ext[0].shape
    grp = _shr(lax.broadcasted_iota(jnp.int32, shape, 1), 6)
    width = _pool_width(grp)
    for t in range(nt):
        x0 = ext[POOL_BUF + t]
        win = _pool_window(x0, lambda j: ext[POOL_BUF + t - j], grp)
        cnt = jnp.minimum(pos0 + t + 1, width).astype(F32)
        d = win / cnt - x0
        o_ref[t] = (_dot(d.astype(BF16), w_ref[...]) * scale_ref[...]).astype(BF16)
    for i in range(POOL_BUF):
        buf_ref[i] = ext[nt + i]


def _pool_sample(u, prev_all, w_bd, layer, scale, *, pos0):
    nt, b, w = u.shape
    return pl.pallas_call(
        functools.partial(_pool_sample_kernel, nt=nt, pos0=pos0),
        grid=(1,),
        in_specs=[pl.BlockSpec((nt, b, w), lambda i: (0, 0, 0)),
                  pl.BlockSpec((None, POOL_BUF, b, w), lambda i: (layer, 0, 0, 0)),
                  pl.BlockSpec((None, w, w), lambda i: (layer, 0, 0)),
                  pl.BlockSpec((1, w), lambda i: (0, 0))],
        out_specs=[pl.BlockSpec((nt, b, w), lambda i: (0, 0, 0)),
                   pl.BlockSpec((POOL_BUF, b, w), lambda i: (0, 0, 0))],
        out_shape=[jax.ShapeDtypeStruct((nt, b, w), BF16),
                   jax.ShapeDtypeStruct((POOL_BUF, b, w), F32)],
        compiler_params=_cparams(("arbitrary",)),
        name="pool_sample",
    )(u, prev_all, w_bd, scale)


def _softplus(z):
    neg_abs = lax.bitcast_convert_type(
        lax.bitcast_convert_type(z, jnp.uint32) | jnp.uint32(0x80000000), F32)
    return jnp.maximum(z, 0.0) + jnp.log(1.0 + jnp.exp(neg_abs))


def _sb_weights(z_t, hi_t, lo_t, nu, carry, mask):
    er = _dot(jnp.concatenate([hi_t, lo_t], axis=1), nu)
    x = z_t + er[:, :SB_TILE] + carry
    if mask is not None:
        x = jnp.where(mask, x, SB_MASKED)
    return jnp.exp(x).astype(BF16), carry + er[:, SB_TILE:]


def _sb_nu():
    r = lax.broadcasted_iota(jnp.int32, (2 * SB_TILE, 2 * SB_TILE), 0) & (SB_TILE - 1)
    c = lax.broadcasted_iota(jnp.int32, (2 * SB_TILE, 2 * SB_TILE), 1)
    return jnp.where((c >= SB_TILE) | (r >= c), -1.0, 0.0).astype(BF16)


def _sb_prompt_kernel(bias_ref, q_ref, k_ref, v_ref, nu_ref, o_ref, carry_ref, acc_ref):
    pair = pl.program_id(1)
    qi = pl.program_id(2)
    T = SB_TILE
    TQ = SB_QBLOCK
    q = q_ref[...]
    nu = nu_ref[...]
    col = lax.broadcasted_iota(jnp.int32, (1, 4 * T), 1)
    bias_row = jnp.where((_shr(col, 7) & 1) == 0, bias_ref[2 * pair], bias_ref[2 * pair + 1])
    head0 = lax.broadcasted_iota(jnp.int32, (LANES, T), 0) < SB_DIM
    colq = lax.broadcasted_iota(jnp.int32, (TQ, 4 * T), 1)
    key_local = (colq & (T - 1)) + jnp.where(colq < 2 * T, T, 0)
    rowq = lax.broadcasted_iota(jnp.int32, (TQ, 4 * T), 0)
    ndiag = TQ // (2 * T)

    def per_head(x):
        zero = jnp.zeros_like(x)
        return [jnp.where(head0, x, zero), jnp.where(head0, zero, x)]

    def block(kb, diag_index):
        masked = diag_index is not None
        if masked:
            diag = key_local + diag_index * 2 * T < rowq
        start = pl.multiple_of(kb * 2 * T, 2 * T)
        kt = k_ref[:, pl.ds(start, 2 * T)]
        vt = v_ref[:, pl.ds(start, 2 * T)]
        k_rhs = jnp.concatenate(per_head(kt[:, T:]) + per_head(kt[:, :T]), axis=1)
        v_rhs = jnp.concatenate(per_head(vt[:, T:]) + per_head(vt[:, :T]), axis=1)
        z = _dot(q, k_rhs) + bias_row
        sp = _softplus(z)
        if masked:
            sp = jnp.where(diag, sp, 0.0)
        hi, lo = _split2(sp)
        carries = [carry_ref[0], carry_ref[1]]
        weights = []
        for t in range(2):
            for h in range(2):
                sl = slice((2 * t + h) * T, (2 * t + h + 1) * T)
                a, carries[h] = _sb_weights(z[:, sl], hi[:, sl], lo[:, sl], nu, carries[h],
                                            diag[:, sl] if masked else None)
                weights.append(a)
        carry_ref[0] = carries[0]
        carry_ref[1] = carries[1]
        acc_ref[...] += _dot_nt(jnp.concatenate(weights, axis=1), v_rhs)

    carry_ref[...] = jnp.zeros_like(carry_ref)
    acc_ref[...] = jnp.zeros_like(acc_ref)
    for d in reversed(range(ndiag)):
        block(qi * ndiag + d, d)

    def body(s, c):
        block(qi * ndiag - 1 - s, None)
        return c

    lax.fori_loop(0, qi * ndiag, body, 0)
    o_ref[...] = acc_ref[...].astype(BF16)


def _sb_prompt(qb, ktb, vtb, bias, seq):
    T = SB_TILE
    TQ = SB_QBLOCK
    batch = ktb.shape[0]
    nq = seq // TQ
    return pl.pallas_call(
        _sb_prompt_kernel,
        grid=(batch, SB_HEADS // 2, nq),
        in_specs=[pl.BlockSpec(memory_space=pltpu.SMEM),
                  pl.BlockSpec((TQ, LANES), lambda b, p, i: (b * nq + i, p)),
                  pl.BlockSpec((None, LANES, seq), lambda b, p, i: (b, p, 0)),
                  pl.BlockSpec((None, LANES, seq), lambda b, p, i: (b, p, 0)),
                  pl.BlockSpec((2 * T, 2 * T), lambda b, p, i: (0, 0))],
        out_specs=pl.BlockSpec((TQ, LANES), lambda b, p, i: (b * nq + i, p)),
        out_shape=jax.ShapeDtypeStruct((batch * seq, SB_WIDTH), BF16),
        scratch_shapes=[pltpu.VMEM((2, TQ, T), F32), pltpu.VMEM((TQ, LANES), F32)],
        compiler_params=_cparams(("arbitrary", "arbitrary", "arbitrary")),
        name="sb_prompt",
    )(bias, qb, ktb, vtb, _sb_nu())


def _sb_sample_kernel(pt_ref, q_ref, kn_ref, vn_ref, bias_ref, nu_ref, *rest, npage, tq):
    k_refs = rest[:npage]
    v_refs = rest[npage:2 * npage]
    o_ref = rest[2 * npage]
    pad_ref = rest[2 * npage + 1]
    T = SB_TILE
    R = SB_HEADS * tq
    q = q_ref[0]
    lane = lax.broadcasted_iota(jnp.int32, (tq, SB_WIDTH), 1)
    qbd = jnp.concatenate(
        [jnp.where(_shr(lane, 6) == h, q, 0.0) for h in range(SB_HEADS)],
        axis=0).astype(BF16)
    bias = bias_ref[...]
    nu = nu_ref[...]

    def padded(ref):
        pad_ref[...] = jnp.zeros((T, SB_WIDTH), F32)
        pad_ref[0:tq, :] = ref[0]
        return pad_ref[...].astype(BF16)

    kn = padded(kn_ref)
    vn = padded(vn_ref)
    kt = jnp.concatenate([r[...].astype(BF16) for r in k_refs], axis=1)
    vt = jnp.concatenate([r[...].astype(BF16) for r in v_refs], axis=1)
    z_all = jnp.concatenate([_dot_nt(qbd, kn), _dot(qbd, kt)], axis=1)
    key = lax.broadcasted_iota(jnp.int32, (R, T), 1)
    qt = lax.broadcasted_iota(jnp.int32, (R, T), 0) & (tq - 1)
    new_mask = key < qt

    carry = jnp.zeros((R, T), F32)
    weights = []
    for t in range(npage + 1):
        z = z_all[:, t * T:(t + 1) * T] + bias
        sp = _softplus(z)
        mask = new_mask if t == 0 else None
        if mask is not None:
            sp = jnp.where(mask, sp, 0.0)
        hi, lo = _split2(sp)
        a, carry = _sb_weights(z, hi, lo, nu, carry, mask)
        weights.append(a)
    acc = _dot(weights[0], vn) + _dot_nt(jnp.concatenate(weights[1:], axis=1), vt)

    out = jnp.zeros((tq, SB_WIDTH), F32)
    for h in range(SB_HEADS):
        out = out + jnp.where(_shr(lane, 6) == h, acc[h * tq:(h + 1) * tq, :], 0.0)
    o_ref[0] = out.astype(BF16)


def _sb_sample(q8, kn8, vn8, bias, cache_kt, cache_vt, page_table, layer):
    b, tq, _ = q8.shape
    npage = page_table.shape[1]
    R = SB_HEADS * tq
    bias_tile = jnp.broadcast_to(jnp.repeat(bias.astype(F32), tq)[:, None], (R, SB_TILE))
    tok = pl.BlockSpec((1, tq, SB_WIDTH), lambda s, pt: (s, 0, 0))

    def page_spec(i):
        return pl.BlockSpec((None, None, SB_WIDTH, PAGE_SIZE),
                            lambda s, pt: (layer, pt[s, npage - 1 - i], 0, 0))

    grid_spec = pltpu.PrefetchScalarGridSpec(
        num_scalar_prefetch=1,
        grid=(b,),
        in_specs=[tok, tok, tok,
                  pl.BlockSpec((R, SB_TILE), lambda s, pt: (0, 0)),
                  pl.BlockSpec((2 * SB_TILE, 2 * SB_TILE), lambda s, pt: (0, 0))]
                 + [page_spec(i) for i in range(npage)]
                 + [page_spec(i) for i in range(npage)],
        out_specs=pl.BlockSpec((1, tq, SB_WIDTH), lambda s, pt: (s, 0, 0)),
        scratch_shapes=[pltpu.VMEM((SB_TILE, SB_WIDTH), F32)],
    )
    return pl.pallas_call(
        functools.partial(_sb_sample_kernel, npage=npage, tq=tq),
        grid_spec=grid_spec,
        out_shape=jax.ShapeDtypeStruct((b, tq, SB_WIDTH), BF16),
        compiler_params=_cparams(("arbitrary",)),
        name="sb_sample",
    )(page_table, q8, kn8, vn8, bias_tile, _sb_nu(),
      *([cache_kt] * npage), *([cache_vt] * npage))


def _out_proj_kernel(og_ref, op_ref, os_ref, x_ref, w_ref, g_ref, b_ref, o_ref, *, alpha):
    mix = (_dot(og_ref[...], w_ref[0:GLA_VAL, :])
           + _dot(op_ref[...], w_ref[GLA_VAL:GLA_VAL + POOL_WIDTH, :])
           + _dot(os_ref[...], w_ref[GLA_VAL + POOL_WIDTH:, :]))
    o_ref[...] = _layer_norm(alpha * x_ref[...] + mix, g_ref[...], b_ref[...])


def _out_proj(og, op, os_, x, w, layer, g, b, alpha):
    n, d = x.shape
    tm = min(n, 512)
    row = lambda c: pl.BlockSpec((tm, c), lambda i: (i, 0))
    const = lambda r, c: pl.BlockSpec((r, c), lambda i: (0, 0))
    return pl.pallas_call(
        functools.partial(_out_proj_kernel, alpha=alpha),
        grid=(n // tm,),
        in_specs=[row(GLA_VAL), row(POOL_WIDTH), row(SB_WIDTH), row(d),
                  pl.BlockSpec((None, w.shape[1], d), lambda i: (layer, 0, 0)),
                  const(1, d), const(1, d)],
        out_specs=row(d),
        out_shape=jax.ShapeDtypeStruct((n, d), F32),
        compiler_params=_cparams(("arbitrary",)),
        name="out_proj_ln",
    )(og, op, os_, x, w, g, b)


def _ffn_kernel(x_ref, wg_ref, wu_ref, wo_ref, g_ref, b_ref, o_ref, acc_ref, xb_ref, *, alpha):
    f = pl.program_id(1)

    @pl.when(f == 0)
    def _():
        acc_ref[...] = jnp.zeros_like(acc_ref)
        xb_ref[...] = x_ref[...].astype(BF16)

    xb = xb_ref[...]
    gate = _dot(xb, wg_ref[...])
    up = _dot(xb, wu_ref[...])
    acc_ref[...] += _dot((_silu(gate) * up).astype(BF16), wo_ref[...])

    @pl.when(f == pl.num_programs(1) - 1)
    def _():
        o_ref[...] = _layer_norm(alpha * x_ref[...] + acc_ref[...], g_ref[...], b_ref[...])


def _ffn(x, w_in, w_out, layer, g, b, alpha, *, tm, tf):
    n, d = x.shape
    hidden = w_out.shape[1]
    nf = hidden // tf
    return pl.pallas_call(
        functools.partial(_ffn_kernel, alpha=alpha),
        grid=(n // tm, nf),
        in_specs=[pl.BlockSpec((tm, d), lambda i, f: (i, 0)),
                  pl.BlockSpec((None, d, tf), lambda i, f: (layer, 0, f)),
                  pl.BlockSpec((None, d, tf), lambda i, f: (layer, 0, nf + f)),
                  pl.BlockSpec((None, tf, d), lambda i, f: (layer, f, 0)),
                  pl.BlockSpec((1, d), lambda i, f: (0, 0)),
                  pl.BlockSpec((1, d), lambda i, f: (0, 0))],
        out_specs=pl.BlockSpec((tm, d), lambda i, f: (i, 0)),
        out_shape=jax.ShapeDtypeStruct((n, d), F32),
        scratch_shapes=[pltpu.VMEM((tm, d), F32), pltpu.VMEM((tm, d), BF16)],
        compiler_params=_cparams(("arbitrary", "arbitrary")),
        name="ffn_ln",
    )(x, w_in, w_in, w_out, g, b)


def _gla_state_to_bd(s):
    b = s.shape[0]
    eye = jnp.eye(GLA_HEADS, dtype=bool)[None, :, None, :, None]
    bd = jnp.where(eye, s[:, :, :, None, :], 0.0)
    return bd.reshape(b, GLA_KEY, GLA_VAL)


def _gla_state_from_bd(bd):
    b = bd.shape[0]
    s = bd.reshape(b, GLA_HEADS, GLA_DK, GLA_HEADS, GLA_DV)
    return jnp.stack([s[:, h, :, h, :] for h in range(GLA_HEADS)], axis=1)


def kernel(x_prompt, x_sample, state_gla, state_pool, cache_k, cache_v, page_table,
           w_in, w_gate_up, b_gate, gla_norm_g, w_pool, pool_scale, sb_bias, w_out,
           ln1_g, ln1_b, w_ffn_in, w_ffn_out, ln2_g, ln2_b):
    depth = w_in.shape[0]
    batch, seq, d_model = x_prompt.shape
    dec_b, dec_t, _ = x_sample.shape
    n_phys = cache_k.shape[1]
    past_len = page_table.shape[1] * PAGE_SIZE
    alpha = (2.0 * depth) ** 0.25
    assert cache_k.shape[2] == PAGE_SIZE and seq % 512 == 0
    assert dec_t <= SUBLANES and dec_b % LANES == 0
    t_pad = SUBLANES

    w_in_t = jnp.transpose(w_in, (0, 2, 1))
    o = 0
    rows = {}
    for name, size in (("q", GLA_KEY), ("k", GLA_KEY), ("v", GLA_VAL), ("a", GATE_RANK),
                       ("g", GLA_VAL), ("u", POOL_WIDTH), ("qs", SB_WIDTH),
                       ("ks", SB_WIDTH), ("vs", SB_WIDTH)):
        rows[name] = w_in_t[:, o:o + size, :]
        o += size
    a_pad = jnp.zeros((depth, LANES - GATE_RANK, d_model), w_in.dtype)
    w_in_p = jnp.concatenate(
        [rows["q"], rows["k"], rows["v"], rows["g"], rows["a"], a_pad,
         rows["u"], rows["qs"], rows["ks"], rows["vs"]], axis=1).astype(BF16)
    wg_p = jnp.pad(w_gate_up, ((0, 0), (0, LANES - GATE_RANK), (0, 0))).astype(BF16)
    ngrp = len(POOL_WINDOWS)
    eye_g = jnp.eye(ngrp, dtype=bool)[None, :, None, :, None]
    w_pool_bd = jnp.where(eye_g, w_pool[:, :, :, None, :], 0.0).reshape(
        depth, POOL_WIDTH, POOL_WIDTH).astype(BF16)
    w_out_b = w_out.astype(BF16)
    w_ffn_in_b = w_ffn_in.astype(BF16)
    w_ffn_out_b = w_ffn_out.astype(BF16)
    cache_kt = jnp.transpose(cache_k, (0, 1, 3, 4, 2)).reshape(depth, n_phys, SB_WIDTH, PAGE_SIZE)
    cache_vt = jnp.transpose(cache_v, (0, 1, 3, 4, 2)).reshape(depth, n_phys, SB_WIDTH, PAGE_SIZE)
    pool_prev = jnp.transpose(state_pool, (0, 2, 1, 3))
    row2 = lambda a: a.reshape(1, -1).astype(F32)

    xp = x_prompt.reshape(batch * seq, d_model)
    xs = jnp.transpose(x_sample, (1, 0, 2)).reshape(dec_t * dec_b, d_model)
    zeros_state = jnp.zeros((batch, GLA_KEY, GLA_VAL), F32)
    outs = {k: [] for k in ("gla_p", "pool_p", "k_p", "v_p", "gla_s", "pool_s", "k_s", "v_s")}

    def seq_major(a):
        a = jnp.transpose(a.reshape(dec_t, dec_b, a.shape[-1]), (1, 0, 2))
        return jnp.pad(a, ((0, 0), (0, t_pad - dec_t), (0, 0)))

    def tok_major(a):
        return jnp.transpose(a[:, :dec_t, :], (1, 0, 2)).reshape(dec_t * dec_b, a.shape[-1])

    for l in range(depth):
        bg, gain = row2(b_gate[l]), row2(gla_norm_g[l])
        pscale = row2(pool_scale[l])
        g1, b1, g2, b2 = row2(ln1_g[l]), row2(ln1_b[l]), row2(ln2_g[l]), row2(ln2_b[l])

        gla_in, u, qb, kt, vt, ktb, vtb = _in_proj(xp, w_in_p, l, groups=batch, tm=512,
                                                   prompt=True)
        o_gla, s_bd = _gla(gla_in.reshape(batch, seq, GLA_IN), zeros_state, wg_p, l, bg, gain,
                           nseq=batch, valid=GLA_CHUNK)
        o_pool, pbuf = _pool_prompt(u.reshape(batch, seq, POOL_WIDTH), w_pool_bd, l, pscale,
                                    tt=512)
        o_sb = _sb_prompt(qb, ktb, vtb, sb_bias[l].astype(F32), seq)
        xp = _out_proj(o_gla.reshape(batch * seq, GLA_VAL), o_pool.reshape(batch * seq, POOL_WIDTH),
                       o_sb, xp, w_out_b, l, g1, b1, alpha)
        xp = _ffn(xp, w_ffn_in_b, w_ffn_out_b, l, g2, b2, alpha, tm=1024, tf=256)
        outs["gla_p"].append(_gla_state_from_bd(s_bd))
        outs["pool_p"].append(pbuf)
        outs["k_p"].append(kt)
        outs["v_p"].append(vt)

        gla_in, u, qb, kt, vt, k, v = _in_proj(xs, w_in_p, l, groups=dec_t, tm=dec_b,
                                               prompt=False)
        o_gla, s_bd = _gla(seq_major(gla_in), _gla_state_to_bd(state_gla[l]), wg_p, l, bg, gain,
                           nseq=8, valid=dec_t)
        o_pool, pbuf = _pool_sample(u.reshape(dec_t, dec_b, POOL_WIDTH), pool_prev, w_pool_bd, l,
                                    pscale, pos0=past_len)
        o_sb = _sb_sample(seq_major(qb.astype(F32)), seq_major(k), seq_major(v), sb_bias[l],
                          cache_kt, cache_vt, page_table, l)
        xs = _out_proj(tok_major(o_gla), o_pool.reshape(dec_t * dec_b, POOL_WIDTH),
                       tok_major(o_sb), xs, w_out_b, l, g1, b1, alpha)
        xs = _ffn(xs, w_ffn_in_b, w_ffn_out_b, l, g2, b2, alpha, tm=512, tf=256)
        outs["gla_s"].append(_gla_state_from_bd(s_bd))
        outs["pool_s"].append(pbuf)
        outs["k_s"].append(kt)
        outs["v_s"].append(vt)

    st = lambda key: jnp.stack(outs[key])
    kv_p = lambda key: jnp.transpose(
        st(key).reshape(depth, batch, SB_HEADS, SB_DIM, seq), (0, 1, 4, 2, 3))
    kv_s = lambda key: jnp.transpose(
        st(key).reshape(depth, dec_t, SB_HEADS, SB_DIM, dec_b), (0, 4, 1, 2, 3))
    y_sample = jnp.transpose(xs.reshape(dec_t, dec_b, d_model), (1, 0, 2))
    return (xp.reshape(batch, seq, d_model), y_sample,
            st("gla_p"), st("pool_p"), kv_p("k_p"), kv_p("v_p"),
            st("gla_s"), jnp.transpose(st("pool_s"), (0, 2, 1, 3)), kv_s("k_s"), kv_s("v_s"))
```

```python
import functools

import jax
import jax.numpy as jnp
from jax import lax
from jax.experimental import pallas as pl
from jax.experimental.pallas import tpu as pltpu

F32 = jnp.float32
BF16 = jnp.bfloat16

GLA_HEADS = 4
GLA_DK = 32
GLA_DV = 64
GLA_KEY = GLA_HEADS * GLA_DK
GLA_VAL = GLA_HEADS * GLA_DV
GATE_RANK = 16
GATE_TAU = 16.0
GLA_CHUNK = 64
POOL_WINDOWS = (2, 4, 8, 16)
POOL_GROUP_DIM = 64
POOL_WIDTH = 256
POOL_BUF = 15
POOL_HALO = 16
SB_HEADS = 8
SB_DIM = 64
SB_WIDTH = SB_HEADS * SB_DIM
SB_TILE = 128
SB_QBLOCK = 4 * SB_TILE
SB_MASKED = -1e30
PAGE_SIZE = 128
LN_EPS = 1e-5

LANES = 128
SUBLANES = 8
VMEM_LIMIT = 56 * 1024 * 1024

GLA_IN = GLA_KEY + GLA_KEY + GLA_VAL + GLA_VAL + LANES
C_GLA = 0
C_U = C_GLA + GLA_IN
C_Q = C_U + POOL_WIDTH
C_K = C_Q + SB_WIDTH
C_V = C_K + SB_WIDTH
C_END = C_V + SB_WIDTH


def _cparams(sem):
    return pltpu.CompilerParams(dimension_semantics=sem, vmem_limit_bytes=VMEM_LIMIT)


def _dot(a, b):
    return jnp.dot(a, b, preferred_element_type=F32)


def _dot_nt(a, b):
    return lax.dot_general(a, b, (((1,), (1,)), ((), ())), preferred_element_type=F32)


def _dot_tn(a, b):
    return lax.dot_general(a, b, (((0,), (0,)), ((), ())), preferred_element_type=F32)


def _split3(x):
    h = x.astype(BF16)
    r = x - h.astype(F32)
    m = r.astype(BF16)
    l = (r - m.astype(F32)).astype(BF16)
    return h, m, l


def _split2(x):
    h = x.astype(BF16)
    l = (x - h.astype(F32)).astype(BF16)
    return h, l


def _shr(x, n):
    return lax.shift_right_logical(x, jnp.int32(n))


def _log_sigmoid(z):
    return jnp.minimum(z, 0.0) - jnp.log(1.0 + jnp.exp(-jnp.abs(z)))


def _silu(x):
    return x / (1.0 + jnp.exp(-x))


def _layer_norm(y, g, b):
    mu = jnp.mean(y, axis=-1, keepdims=True)
    d = y - mu
    var = jnp.mean(d * d, axis=-1, keepdims=True)
    return d * lax.rsqrt(var + LN_EPS) * g + b


def _in_proj_kernel(x_ref, w_ref, *refs, prompt, aliased):
    if aliased:
        refs = refs[2:]
    gla_ref, u_ref, qb_ref, kt_ref, vt_ref, *extra = refs
    xb = x_ref[...].astype(BF16)
    nat = lambda lo, hi: _dot_nt(xb, w_ref[lo:hi, :])
    gla_ref[...] = nat(C_GLA, C_U)
    u_ref[...] = nat(C_U, C_Q)
    qb_ref[...] = (nat(C_Q, C_K) * (SB_DIM ** -0.5)).astype(BF16)
    kt = _dot_nt(w_ref[C_K:C_V, :], xb)
    vt = _dot_nt(w_ref[C_V:C_END, :], xb)
    kt_ref[...] = kt
    vt_ref[...] = vt
    if prompt:
        ktb_ref, vtb_ref = extra
        ktb_ref[...] = kt.astype(BF16)
        vtb_ref[...] = vt.astype(BF16)
    else:
        k_ref, v_ref = extra
        k_ref[...] = nat(C_K, C_V)
        v_ref[...] = nat(C_V, C_END)


def _in_proj(x, w_t, layer, depth, kv_stack, *, groups, tm, prompt):
    n, d = x.shape
    glen = n // groups
    per = glen // tm
    row = lambda c: pl.BlockSpec((tm, c), lambda i: (i, 0))
    ft = pl.BlockSpec((None, SB_WIDTH, tm), lambda i: (i // per, 0, i % per))
    ft_shape = lambda dt: jax.ShapeDtypeStruct((groups, SB_WIDTH, glen), dt)
    stack = pl.BlockSpec((None, None, SB_WIDTH, tm), lambda i: (layer, i // per, 0, i % per))
    stack_shape = jax.ShapeDtypeStruct((depth, groups, SB_WIDTH, glen), F32)
    nat_shape = lambda c, dt: jax.ShapeDtypeStruct((n, c), dt)
    out_specs = [row(GLA_IN), row(POOL_WIDTH), row(SB_WIDTH), stack, stack]
    out_shape = [nat_shape(GLA_IN, F32), nat_shape(POOL_WIDTH, F32),
                 nat_shape(SB_WIDTH, BF16), stack_shape, stack_shape]
    if prompt:
        out_specs += [ft, ft]
        out_shape += [ft_shape(BF16), ft_shape(BF16)]
    else:
        out_specs += [row(SB_WIDTH), row(SB_WIDTH)]
        out_shape += [nat_shape(SB_WIDTH, F32), nat_shape(SB_WIDTH, F32)]
    in_specs = [row(d), pl.BlockSpec((None, C_END, d), lambda i: (layer, 0, 0))]
    operands = [x, w_t]
    aliases = {}
    if kv_stack is not None:
        in_specs += [pl.BlockSpec(memory_space=pl.ANY)] * 2
        operands += list(kv_stack)
        aliases = {2: 3, 3: 4}
    return pl.pallas_call(
        functools.partial(_in_proj_kernel, prompt=prompt, aliased=kv_stack is not None),
        grid=(n // tm,),
        in_specs=in_specs,
        out_specs=out_specs,
        out_shape=out_shape,
        input_output_aliases=aliases,
        compiler_params=_cparams(("arbitrary",)),
        name="in_proj",
    )(*operands)


def _gla_kernel(x_ref, s0_ref, wg_ref, bg_ref, gain_ref, tri_ref, hmean_ref,
                o_ref, s_ref, *, nseq, t_blk, valid):
    L = GLA_CHUNK
    c = pl.program_id(1)

    @pl.when(c == 0)
    def _():
        s_ref[...] = s0_ref[...]

    row = lax.broadcasted_iota(jnp.int32, (L, GLA_KEY), 0)
    lane_k = lax.broadcasted_iota(jnp.int32, (L, GLA_KEY), 1)
    tq = lax.broadcasted_iota(jnp.int32, (GLA_HEADS * L, L), 0)
    ts = lax.broadcasted_iota(jnp.int32, (GLA_HEADS * L, L), 1)
    causal = (tq & (L - 1)) >= ts
    lane_v = lax.broadcasted_iota(jnp.int32, (L, GLA_VAL), 1)
    srow = lax.broadcasted_iota(jnp.int32, (GLA_KEY, GLA_VAL), 0)
    scol = lax.broadcasted_iota(jnp.int32, (GLA_KEY, GLA_VAL), 1)
    sdiag = _shr(srow, 5) == _shr(scol, 6)
    tri = tri_ref[...]
    hmean = hmean_ref[...]

    nsub = max(t_blk // L, 1)
    items = [(i, sub) for i in range(nseq) for sub in range(nsub)]
    dot3 = lambda x, w: (lambda h, m, l: _dot(h, w) + _dot(m, w) + _dot(l, w))(*_split3(x))

    blks, rows_of = [], []
    for i, sub in items:
        if t_blk < L:
            blks.append(jnp.concatenate([x_ref[i], jnp.zeros((L - t_blk, GLA_IN), F32)], axis=0))
            rows_of.append(slice(0, t_blk))
        else:
            rows_of.append(slice(sub * L, (sub + 1) * L))
            blks.append(x_ref[i, rows_of[-1], :])
    q = [blk[:, 0:GLA_KEY] * (GLA_DK ** -0.5) for blk in blks]
    k = [blk[:, GLA_KEY:2 * GLA_KEY] for blk in blks]
    vb = [blk[:, 2 * GLA_KEY:2 * GLA_KEY + GLA_VAL].astype(BF16) for blk in blks]
    g = [blk[:, 2 * GLA_KEY + GLA_VAL:2 * GLA_KEY + 2 * GLA_VAL] for blk in blks]

    z = [_dot(blk[:, 2 * GLA_KEY + 2 * GLA_VAL:GLA_IN].astype(BF16), wg_ref[...]) + bg_ref[...]
         for blk in blks]
    log_a = [_log_sigmoid(zi) * (1.0 / GATE_TAU) for zi in z]
    if valid < L:
        log_a = [jnp.where(row < valid, la, 0.0) for la in log_a]
    b = [(lambda h, m, l: _dot(tri, h) + _dot(tri, m) + _dot(tri, l))(*_split3(la))
         for la in log_a]
    b_last = [bi[L - 1:L, :] for bi in b]
    b_mid = [bi[L // 2 - 1:L // 2, :] for bi in b]

    def stack_heads(qm):
        return jnp.concatenate(
            [jnp.where(_shr(lane_k, 5) == h, qm, 0.0) for h in range(GLA_HEADS)],
            axis=0).astype(BF16)

    scores = [jnp.where(causal,
                        _dot_nt(stack_heads(q[n] * jnp.exp(b[n] - b_mid[n])),
                                (k[n] * jnp.exp(b_mid[n] - b[n])).astype(BF16)), 0.0)
              for n in range(len(items))]
    upd = [jnp.where(sdiag, _dot_tn((k[n] * jnp.exp(b_last[n] - b[n])).astype(BF16), vb[n]), 0.0)
           for n in range(len(items))]
    p = [_dot(scores[n].astype(BF16), vb[n]) for n in range(len(items))]

    o = []
    for n, (i, sub) in enumerate(items):
        if sub == 0:
            s_bd = s_ref[i]
        o.append(_dot((q[n] * jnp.exp(b[n])).astype(BF16), s_bd.astype(BF16)))
        dcol = jnp.transpose(jnp.broadcast_to(jnp.exp(b_last[n]), (GLA_KEY, GLA_KEY)))
        s_bd = jnp.concatenate([dcol, dcol], axis=1) * s_bd + upd[n]
        if sub == nsub - 1:
            s_ref[i] = s_bd
    for n in range(len(items)):
        for h in range(GLA_HEADS):
            o[n] = o[n] + jnp.where(_shr(lane_v, 6) == h, p[n][h * L:(h + 1) * L, :], 0.0)

    d = [o[n] - dot3(o[n], hmean) for n in range(len(items))]
    var = [dot3(dn * dn, hmean) for dn in d]
    for n, (i, sub) in enumerate(items):
        y = d[n] * lax.rsqrt(var[n] + LN_EPS) * gain_ref[...] * _silu(g[n])
        rows = rows_of[n]
        o_ref[i, rows, :] = y[0:rows.stop - rows.start, :].astype(BF16)


def _gla(x, s0_bd, wg, layer, bg, gain, *, nseq, nsub, valid):
    b, t, _ = x.shape
    L = GLA_CHUNK
    t_blk = nsub * L if t >= L else t
    nchunk = t // t_blk
    tri = (lax.broadcasted_iota(jnp.int32, (L, L), 0)
           >= lax.broadcasted_iota(jnp.int32, (L, L), 1)).astype(BF16)
    hm = ((lax.broadcasted_iota(jnp.int32, (GLA_VAL, GLA_VAL), 0) // GLA_DV)
          == (lax.broadcasted_iota(jnp.int32, (GLA_VAL, GLA_VAL), 1) // GLA_DV))
    hmean = jnp.where(hm, 1.0 / GLA_DV, 0.0).astype(BF16)
    const = lambda shape: pl.BlockSpec(shape, lambda s, c: (0,) * len(shape))
    return pl.pallas_call(
        functools.partial(_gla_kernel, nseq=nseq, t_blk=t_blk, valid=valid),
        grid=(b // nseq, nchunk),
        in_specs=[pl.BlockSpec((nseq, t_blk, GLA_IN), lambda s, c: (s, c, 0)),
                  pl.BlockSpec((nseq, GLA_KEY, GLA_VAL), lambda s, c: (s, 0, 0)),
                  pl.BlockSpec((None, LANES, GLA_KEY), lambda s, c: (layer, 0, 0)),
                  const((1, GLA_KEY)), const((1, GLA_VAL)),
                  const((L, L)), const((GLA_VAL, GLA_VAL))],
        out_specs=[pl.BlockSpec((nseq, t_blk, GLA_VAL), lambda s, c: (s, c, 0)),
                   pl.BlockSpec((nseq, GLA_KEY, GLA_VAL), lambda s, c: (s, 0, 0))],
        out_shape=[jax.ShapeDtypeStruct((b, t, GLA_VAL), BF16),
                   jax.ShapeDtypeStruct((b, GLA_KEY, GLA_VAL), F32)],
        compiler_params=_cparams(("arbitrary", "arbitrary")),
        name="gla",
    )(x, s0_bd, wg, bg, gain, tri, hmean)


def _pool_window(x0, sh, grp):
    s2 = x0 + sh(1)
    s4 = s2 + sh(2) + sh(3)
    s8 = s4 + sh(4) + sh(5) + sh(6) + sh(7)
    s16 = s8 + sh(8) + sh(9) + sh(10) + sh(11) + sh(12) + sh(13) + sh(14) + sh(15)
    return jnp.where(grp == 0, s2, jnp.where(grp == 1, s4, jnp.where(grp == 2, s8, s16)))


def _pool_width(grp):
    return jnp.where(grp == 0, POOL_WINDOWS[0],
                     jnp.where(grp == 1, POOL_WINDOWS[1],
                               jnp.where(grp == 2, POOL_WINDOWS[2], POOL_WINDOWS[3])))


def _pool_prompt_kernel(u_ref, w_ref, scale_ref, o_ref, buf_ref, x_scr, *, tt):
    t = pl.program_id(1)
    H = POOL_HALO

    @pl.when(t == 0)
    def _():
        x_scr[0:H, :] = jnp.zeros((H, POOL_WIDTH), F32)

    x_scr[H:H + tt, :] = u_ref[...]
    x0 = x_scr[H:H + tt, :]
    shape = (tt, POOL_WIDTH)
    grp = _shr(lax.broadcasted_iota(jnp.int32, shape, 1), 6)
    win = _pool_window(x0, lambda j: x_scr[H - j:H - j + tt, :], grp)
    pos = t * tt + lax.broadcasted_iota(jnp.int32, shape, 0)
    cnt = jnp.minimum(pos + 1, _pool_width(grp)).astype(F32)
    d = win / cnt - x0
    o_ref[...] = (_dot(d.astype(BF16), w_ref[...]) * scale_ref[...]).astype(BF16)

    @pl.when(t == pl.num_programs(1) - 1)
    def _():
        buf_ref[...] = x_scr[H + tt - POOL_BUF:H + tt, :]

    x_scr[0:H, :] = x_scr[tt:tt + H, :]


def _pool_prompt(u, w_bd, layer, scale, *, tt):
    b, t, w = u.shape
    return pl.pallas_call(
        functools.partial(_pool_prompt_kernel, tt=tt),
        grid=(b, t // tt),
        in_specs=[pl.BlockSpec((None, tt, w), lambda s, i: (s, i, 0)),
                  pl.BlockSpec((None, w, w), lambda s, i: (layer, 0, 0)),
                  pl.BlockSpec((1, w), lambda s, i: (0, 0))],
        out_specs=[pl.BlockSpec((None, tt, w), lambda s, i: (s, i, 0)),
                   pl.BlockSpec((None, POOL_BUF, w), lambda s, i: (s, 0, 0))],
        out_shape=[jax.ShapeDtypeStruct((b, t, w), BF16),
                   jax.ShapeDtypeStruct((b, POOL_BUF, w), F32)],
        scratch_shapes=[pltpu.VMEM((POOL_HALO + tt, w), F32)],
        compiler_params=_cparams(("arbitrary", "arbitrary")),
        name="pool_prompt",
    )(u, w_bd, scale)


def _pool_sample_kernel(u_ref, prev_ref, w_ref, scale_ref, o_ref, buf_ref, *, nt, pos0):
    ext = [prev_ref[i] for i in range(POOL_BUF)] + [u_ref[t] for t in range(nt)]
    shape = ext[0].shape
    grp = _shr(lax.broadcasted_iota(jnp.int32, shape, 1), 6)
    width = _pool_width(grp)
    for t in range(nt):
        x0 = ext[POOL_BUF + t]
        win = _pool_window(x0, lambda j: ext[POOL_BUF + t - j], grp)
        cnt = jnp.minimum(pos0 + t + 1, width).astype(F32)
        d = win / cnt - x0
        o_ref[t] = (_dot(d.astype(BF16), w_ref[...]) * scale_ref[...]).astype(BF16)
    for i in range(POOL_BUF):
        buf_ref[i] = ext[nt + i]


def _pool_sample(u, prev_all, w_bd, layer, scale, *, pos0):
    nt, b, w = u.shape
    return pl.pallas_call(
        functools.partial(_pool_sample_kernel, nt=nt, pos0=pos0),
        grid=(1,),
        in_specs=[pl.BlockSpec((nt, b, w), lambda i: (0, 0, 0)),
                  pl.BlockSpec((None, POOL_BUF, b, w), lambda i: (layer, 0, 0, 0)),
                  pl.BlockSpec((None, w, w), lambda i: (layer, 0, 0)),
                  pl.BlockSpec((1, w), lambda i: (0, 0))],
        out_specs=[pl.BlockSpec((nt, b, w), lambda i: (0, 0, 0)),
                   pl.BlockSpec((POOL_BUF, b, w), lambda i: (0, 0, 0))],
        out_shape=[jax.ShapeDtypeStruct((nt, b, w), BF16),
                   jax.ShapeDtypeStruct((POOL_BUF, b, w), F32)],
        compiler_params=_cparams(("arbitrary",)),
        name="pool_sample",
    )(u, prev_all, w_bd, scale)


def _softplus(z):
    neg_abs = lax.bitcast_convert_type(
        lax.bitcast_convert_type(z, jnp.uint32) | jnp.uint32(0x80000000), F32)
    return jnp.maximum(z, 0.0) + jnp.log(1.0 + jnp.exp(neg_abs))


def _sb_weights(z_t, hi_t, lo_t, nu, carry, mask):
    return _sb_weights_from(z_t, _sb_suffix(hi_t, lo_t, nu), carry, mask)


def _sb_suffix(hi_t, lo_t, nu):
    return _dot(jnp.concatenate([hi_t, lo_t], axis=1), nu)


def _sb_weights_from(z_t, er, carry, mask):
    x = z_t + er[:, :SB_TILE] + carry
    if mask is not None:
        x = jnp.where(mask, x, SB_MASKED)
    return jnp.exp(x).astype(BF16), carry + er[:, SB_TILE:]


def _sb_nu():
    r = lax.broadcasted_iota(jnp.int32, (2 * SB_TILE, 2 * SB_TILE), 0) & (SB_TILE - 1)
    c = lax.broadcasted_iota(jnp.int32, (2 * SB_TILE, 2 * SB_TILE), 1)
    return jnp.where((c >= SB_TILE) | (r >= c), -1.0, 0.0).astype(BF16)


def _sb_prompt_kernel(bias_ref, q_ref, k_ref, v_ref, nu_ref, o_ref, carry_ref, acc_ref):
    pair = pl.program_id(1)
    qi = pl.program_id(2)
    T = SB_TILE
    TQ = SB_QBLOCK
    q = q_ref[...]
    nu = nu_ref[...]
    col = lax.broadcasted_iota(jnp.int32, (1, 4 * T), 1)
    bias_row = jnp.where((_shr(col, 7) & 1) == 0, bias_ref[2 * pair], bias_ref[2 * pair + 1])
    head0 = lax.broadcasted_iota(jnp.int32, (LANES, T), 0) < SB_DIM
    colq = lax.broadcasted_iota(jnp.int32, (TQ, 4 * T), 1)
    key_local = (colq & (T - 1)) + jnp.where(colq < 2 * T, T, 0)
    rowq = lax.broadcasted_iota(jnp.int32, (TQ, 4 * T), 0)
    ndiag = TQ // (2 * T)

    def per_head(x):
        zero = jnp.zeros_like(x)
        return [jnp.where(head0, x, zero), jnp.where(head0, zero, x)]

    def run(blocks):
        zs, v_rhss, diags = [], [], []
        for kb, diag_index in blocks:
            start = pl.multiple_of(kb * 2 * T, 2 * T)
            kt = k_ref[:, pl.ds(start, 2 * T)]
            vt = v_ref[:, pl.ds(start, 2 * T)]
            k_rhs = jnp.concatenate(per_head(kt[:, T:]) + per_head(kt[:, :T]), axis=1)
            v_rhss.append(jnp.concatenate(per_head(vt[:, T:]) + per_head(vt[:, :T]), axis=1))
            zs.append(_dot(q, k_rhs) + bias_row)
            diags.append(None if diag_index is None
                         else key_local + diag_index * 2 * T < rowq)
        tiles = [slice(c * T, (c + 1) * T) for c in range(4)]
        ers = []
        for z, diag in zip(zs, diags):
            sp = _softplus(z)
            if diag is not None:
                sp = jnp.where(diag, sp, 0.0)
            hi, lo = _split2(sp)
            ers.append([_sb_suffix(hi[:, sl], lo[:, sl], nu) for sl in tiles])
        carries = [carry_ref[0], carry_ref[1]]
        acc = acc_ref[...]
        for z, diag, er, v_rhs in zip(zs, diags, ers, v_rhss):
            weights = []
            for c, sl in enumerate(tiles):
                a, carries[c % 2] = _sb_weights_from(
                    z[:, sl], er[c], carries[c % 2], None if diag is None else diag[:, sl])
                weights.append(a)
            acc = acc + _dot_nt(jnp.concatenate(weights, axis=1), v_rhs)
        carry_ref[0] = carries[0]
        carry_ref[1] = carries[1]
        acc_ref[...] = acc

    carry_ref[...] = jnp.zeros_like(carry_ref)
    acc_ref[...] = jnp.zeros_like(acc_ref)
    run([(qi * ndiag + d, d) for d in reversed(range(ndiag))])

    def body(s, c):
        first = qi * ndiag - 1 - s * ndiag
        run([(first - u, None) for u in range(ndiag)])
        return c

    lax.fori_loop(0, qi, body, 0)
    o_ref[...] = acc_ref[...].astype(BF16)


def _sb_prompt(qb, ktb, vtb, bias, seq):
    T = SB_TILE
    TQ = SB_QBLOCK
    batch = ktb.shape[0]
    nq = seq // TQ
    return pl.pallas_call(
        _sb_prompt_kernel,
        grid=(batch, SB_HEADS // 2, nq),
        in_specs=[pl.BlockSpec(memory_space=pltpu.SMEM),
                  pl.BlockSpec((TQ, LANES), lambda b, p, i: (b * nq + i, p)),
                  pl.BlockSpec((None, LANES, seq), lambda b, p, i: (b, p, 0)),
                  pl.BlockSpec((None, LANES, seq), lambda b, p, i: (b, p, 0)),
                  pl.BlockSpec((2 * T, 2 * T), lambda b, p, i: (0, 0))],
        out_specs=pl.BlockSpec((TQ, LANES), lambda b, p, i: (b * nq + i, p)),
        out_shape=jax.ShapeDtypeStruct((batch * seq, SB_WIDTH), BF16),
        scratch_shapes=[pltpu.VMEM((2, TQ, T), F32), pltpu.VMEM((TQ, LANES), F32)],
        compiler_params=_cparams(("arbitrary", "arbitrary", "arbitrary")),
        name="sb_prompt",
    )(bias, qb, ktb, vtb, _sb_nu())


def _sb_sample_kernel(pt_ref, q_ref, kn_ref, vn_ref, bias_ref, nu_ref, *rest, npage, tq):
    k_refs = rest[:npage]
    v_refs = rest[npage:2 * npage]
    o_ref = rest[2 * npage]
    pad_ref = rest[2 * npage + 1]
    T = SB_TILE
    R = SB_HEADS * tq
    q = q_ref[0]
    lane = lax.broadcasted_iota(jnp.int32, (tq, SB_WIDTH), 1)
    qbd = jnp.concatenate(
        [jnp.where(_shr(lane, 6) == h, q, 0.0) for h in range(SB_HEADS)],
        axis=0).astype(BF16)
    bias = bias_ref[...]
    nu = nu_ref[...]

    def padded(ref):
        pad_ref[...] = jnp.zeros((T, SB_WIDTH), F32)
        pad_ref[0:tq, :] = ref[0]
        return pad_ref[...].astype(BF16)

    kn = padded(kn_ref)
    vn = padded(vn_ref)
    kt = jnp.concatenate([r[...].astype(BF16) for r in k_refs], axis=1)
    vt = jnp.concatenate([r[...].astype(BF16) for r in v_refs], axis=1)
    z_all = jnp.concatenate([_dot_nt(qbd, kn), _dot(qbd, kt)], axis=1)
    key = lax.broadcasted_iota(jnp.int32, (R, T), 1)
    qt = lax.broadcasted_iota(jnp.int32, (R, T), 0) & (tq - 1)
    new_mask = key < qt

    carry = jnp.zeros((R, T), F32)
    weights = []
    for t in range(npage + 1):
        z = z_all[:, t * T:(t + 1) * T] + bias
        sp = _softplus(z)
        mask = new_mask if t == 0 else None
        if mask is not None:
            sp = jnp.where(mask, sp, 0.0)
        hi, lo = _split2(sp)
        a, carry = _sb_weights(z, hi, lo, nu, carry, mask)
        weights.append(a)
    acc = _dot(weights[0], vn) + _dot_nt(jnp.concatenate(weights[1:], axis=1), vt)

    out = jnp.zeros((tq, SB_WIDTH), F32)
    for h in range(SB_HEADS):
        out = out + jnp.where(_shr(lane, 6) == h, acc[h * tq:(h + 1) * tq, :], 0.0)
    o_ref[0] = out.astype(BF16)


def _sb_sample(q8, kn8, vn8, bias, cache_kt, cache_vt, page_table, layer):
    b, tq, _ = q8.shape
    npage = page_table.shape[1]
    R = SB_HEADS * tq
    bias_tile = jnp.broadcast_to(jnp.repeat(bias.astype(F32), tq)[:, None], (R, SB_TILE))
    tok = pl.BlockSpec((1, tq, SB_WIDTH), lambda s, pt: (s, 0, 0))

    def page_spec(i):
        return pl.BlockSpec((None, None, SB_WIDTH, PAGE_SIZE),
                            lambda s, pt: (layer, pt[s, npage - 1 - i], 0, 0))

    grid_spec = pltpu.PrefetchScalarGridSpec(
        num_scalar_prefetch=1,
        grid=(b,),
        in_specs=[tok, tok, tok,
                  pl.BlockSpec((R, SB_TILE), lambda s, pt: (0, 0)),
                  pl.BlockSpec((2 * SB_TILE, 2 * SB_TILE), lambda s, pt: (0, 0))]
                 + [page_spec(i) for i in range(npage)]
                 + [page_spec(i) for i in range(npage)],
        out_specs=pl.BlockSpec((1, tq, SB_WIDTH), lambda s, pt: (s, 0, 0)),
        scratch_shapes=[pltpu.VMEM((SB_TILE, SB_WIDTH), F32)],
    )
    return pl.pallas_call(
        functools.partial(_sb_sample_kernel, npage=npage, tq=tq),
        grid_spec=grid_spec,
        out_shape=jax.ShapeDtypeStruct((b, tq, SB_WIDTH), BF16),
        compiler_params=_cparams(("arbitrary",)),
        name="sb_sample",
    )(page_table, q8, kn8, vn8, bias_tile, _sb_nu(),
      *([cache_kt] * npage), *([cache_vt] * npage))


def _out_proj_kernel(og_ref, op_ref, os_ref, x_ref, w_ref, g_ref, b_ref, o_ref, *, alpha):
    mix = (_dot(og_ref[...], w_ref[0:GLA_VAL, :])
           + _dot(op_ref[...], w_ref[GLA_VAL:GLA_VAL + POOL_WIDTH, :])
           + _dot(os_ref[...], w_ref[GLA_VAL + POOL_WIDTH:, :]))
    o_ref[...] = _layer_norm(alpha * x_ref[...] + mix, g_ref[...], b_ref[...])


def _out_proj(og, op, os_, x, w, layer, g, b, alpha):
    n, d = x.shape
    tm = min(n, 512)
    row = lambda c: pl.BlockSpec((tm, c), lambda i: (i, 0))
    const = lambda r, c: pl.BlockSpec((r, c), lambda i: (0, 0))
    return pl.pallas_call(
        functools.partial(_out_proj_kernel, alpha=alpha),
        grid=(n // tm,),
        in_specs=[row(GLA_VAL), row(POOL_WIDTH), row(SB_WIDTH), row(d),
                  pl.BlockSpec((None, w.shape[1], d), lambda i: (layer, 0, 0)),
                  const(1, d), const(1, d)],
        out_specs=row(d),
        out_shape=jax.ShapeDtypeStruct((n, d), F32),
        compiler_params=_cparams(("arbitrary",)),
        name="out_proj_ln",
    )(og, op, os_, x, w, g, b)


def _ffn_kernel(x_ref, wg_ref, wu_ref, wo_ref, g_ref, b_ref, o_ref, acc_ref, xb_ref, *, alpha):
    f = pl.program_id(1)

    @pl.when(f == 0)
    def _():
        acc_ref[...] = jnp.zeros_like(acc_ref)
        xb_ref[...] = x_ref[...].astype(BF16)

    xb = xb_ref[...]
    gate = _dot(xb, wg_ref[...])
    up = _dot(xb, wu_ref[...])
    acc_ref[...] += _dot((_silu(gate) * up).astype(BF16), wo_ref[...])

    @pl.when(f == pl.num_programs(1) - 1)
    def _():
        o_ref[...] = _layer_norm(alpha * x_ref[...] + acc_ref[...], g_ref[...], b_ref[...])


def _ffn(x, w_in, w_out, layer, g, b, alpha, *, tm, tf):
    n, d = x.shape
    hidden = w_out.shape[1]
    nf = hidden // tf
    return pl.pallas_call(
        functools.partial(_ffn_kernel, alpha=alpha),
        grid=(n // tm, nf),
        in_specs=[pl.BlockSpec((tm, d), lambda i, f: (i, 0)),
                  pl.BlockSpec((None, d, tf), lambda i, f: (layer, 0, f)),
                  pl.BlockSpec((None, d, tf), lambda i, f: (layer, 0, nf + f)),
                  pl.BlockSpec((None, tf, d), lambda i, f: (layer, f, 0)),
                  pl.BlockSpec((1, d), lambda i, f: (0, 0)),
                  pl.BlockSpec((1, d), lambda i, f: (0, 0))],
        out_specs=pl.BlockSpec((tm, d), lambda i, f: (i, 0)),
        out_shape=jax.ShapeDtypeStruct((n, d), F32),
        scratch_shapes=[pltpu.VMEM((tm, d), F32), pltpu.VMEM((tm, d), BF16)],
        compiler_params=_cparams(("arbitrary", "arbitrary")),
        name="ffn_ln",
    )(x, w_in, w_in, w_out, g, b)


def _gla_state_to_bd(s):
    b = s.shape[0]
    eye = jnp.eye(GLA_HEADS, dtype=bool)[None, :, None, :, None]
    bd = jnp.where(eye, s[:, :, :, None, :], 0.0)
    return bd.reshape(b, GLA_KEY, GLA_VAL)


def _gla_state_from_bd(bd):
    b = bd.shape[0]
    s = bd.reshape(b, GLA_HEADS, GLA_DK, GLA_HEADS, GLA_DV)
    return jnp.stack([s[:, h, :, h, :] for h in range(GLA_HEADS)], axis=1)


def kernel(x_prompt, x_sample, state_gla, state_pool, cache_k, cache_v, page_table,
           w_in, w_gate_up, b_gate, gla_norm_g, w_pool, pool_scale, sb_bias, w_out,
           ln1_g, ln1_b, w_ffn_in, w_ffn_out, ln2_g, ln2_b):
    depth = w_in.shape[0]
    batch, seq, d_model = x_prompt.shape
    dec_b, dec_t, _ = x_sample.shape
    n_phys = cache_k.shape[1]
    past_len = page_table.shape[1] * PAGE_SIZE
    alpha = (2.0 * depth) ** 0.25
    assert cache_k.shape[2] == PAGE_SIZE and seq % 512 == 0
    assert dec_t <= SUBLANES and dec_b % LANES == 0
    t_pad = SUBLANES

    w_in_t = jnp.transpose(w_in, (0, 2, 1))
    o = 0
    rows = {}
    for name, size in (("q", GLA_KEY), ("k", GLA_KEY), ("v", GLA_VAL), ("a", GATE_RANK),
                       ("g", GLA_VAL), ("u", POOL_WIDTH), ("qs", SB_WIDTH),
                       ("ks", SB_WIDTH), ("vs", SB_WIDTH)):
        rows[name] = w_in_t[:, o:o + size, :]
        o += size
    a_pad = jnp.zeros((depth, LANES - GATE_RANK, d_model), w_in.dtype)
    w_in_p = jnp.concatenate(
        [rows["q"], rows["k"], rows["v"], rows["g"], rows["a"], a_pad,
         rows["u"], rows["qs"], rows["ks"], rows["vs"]], axis=1).astype(BF16)
    wg_p = jnp.pad(w_gate_up, ((0, 0), (0, LANES - GATE_RANK), (0, 0))).astype(BF16)
    ngrp = len(POOL_WINDOWS)
    eye_g = jnp.eye(ngrp, dtype=bool)[None, :, None, :, None]
    w_pool_bd = jnp.where(eye_g, w_pool[:, :, :, None, :], 0.0).reshape(
        depth, POOL_WIDTH, POOL_WIDTH).astype(BF16)
    w_out_b = w_out.astype(BF16)
    w_ffn_in_b = w_ffn_in.astype(BF16)
    w_ffn_out_b = w_ffn_out.astype(BF16)
    cache_kt = jnp.transpose(cache_k, (0, 1, 3, 4, 2)).reshape(depth, n_phys, SB_WIDTH, PAGE_SIZE)
    cache_vt = jnp.transpose(cache_v, (0, 1, 3, 4, 2)).reshape(depth, n_phys, SB_WIDTH, PAGE_SIZE)
    pool_prev = jnp.transpose(state_pool, (0, 2, 1, 3))
    row2 = lambda a: a.reshape(1, -1).astype(F32)

    xp = x_prompt.reshape(batch * seq, d_model)
    xs = jnp.transpose(x_sample, (1, 0, 2)).reshape(dec_t * dec_b, d_model)
    zeros_state = jnp.zeros((batch, GLA_KEY, GLA_VAL), F32)
    outs = {k: [] for k in ("gla_p", "pool_p", "gla_s", "pool_s")}
    kv_stack_p = kv_stack_s = None

    def seq_major(a):
        a = jnp.transpose(a.reshape(dec_t, dec_b, a.shape[-1]), (1, 0, 2))
        return jnp.pad(a, ((0, 0), (0, t_pad - dec_t), (0, 0)))

    def tok_major(a):
        return jnp.transpose(a[:, :dec_t, :], (1, 0, 2)).reshape(dec_t * dec_b, a.shape[-1])

    for l in range(depth):
        bg, gain = row2(b_gate[l]), row2(gla_norm_g[l])
        pscale = row2(pool_scale[l])
        g1, b1, g2, b2 = row2(ln1_g[l]), row2(ln1_b[l]), row2(ln2_g[l]), row2(ln2_b[l])

        gla_in, u, qb, *kv_stack_p, ktb, vtb = _in_proj(xp, w_in_p, l, depth, kv_stack_p,
                                                        groups=batch, tm=512, prompt=True)
        o_gla, s_bd = _gla(gla_in.reshape(batch, seq, GLA_IN), zeros_state, wg_p, l, bg, gain,
                           nseq=batch, nsub=4, valid=GLA_CHUNK)
        o_pool, pbuf = _pool_prompt(u.reshape(batch, seq, POOL_WIDTH), w_pool_bd, l, pscale,
                                    tt=512)
        o_sb = _sb_prompt(qb, ktb, vtb, sb_bias[l].astype(F32), seq)
        xp = _out_proj(o_gla.reshape(batch * seq, GLA_VAL), o_pool.reshape(batch * seq, POOL_WIDTH),
                       o_sb, xp, w_out_b, l, g1, b1, alpha)
        xp = _ffn(xp, w_ffn_in_b, w_ffn_out_b, l, g2, b2, alpha, tm=1024, tf=256)
        outs["gla_p"].append(_gla_state_from_bd(s_bd))
        outs["pool_p"].append(pbuf)

        gla_in, u, qb, *kv_stack_s, k, v = _in_proj(xs, w_in_p, l, depth, kv_stack_s,
                                                    groups=dec_t, tm=dec_b, prompt=False)
        o_gla, s_bd = _gla(seq_major(gla_in), _gla_state_to_bd(state_gla[l]), wg_p, l, bg, gain,
                           nseq=16, nsub=1, valid=dec_t)
        o_pool, pbuf = _pool_sample(u.reshape(dec_t, dec_b, POOL_WIDTH), pool_prev, w_pool_bd, l,
                                    pscale, pos0=past_len)
        o_sb = _sb_sample(seq_major(qb.astype(F32)), seq_major(k), seq_major(v), sb_bias[l],
                          cache_kt, cache_vt, page_table, l)
        xs = _out_proj(tok_major(o_gla), o_pool.reshape(dec_t * dec_b, POOL_WIDTH),
                       tok_major(o_sb), xs, w_out_b, l, g1, b1, alpha)
        xs = _ffn(xs, w_ffn_in_b, w_ffn_out_b, l, g2, b2, alpha, tm=512, tf=256)
        outs["gla_s"].append(_gla_state_from_bd(s_bd))
        outs["pool_s"].append(pbuf)

    st = lambda key: jnp.stack(outs[key])
    kv_p = lambda a: jnp.transpose(
        a.reshape(depth, batch, SB_HEADS, SB_DIM, seq), (0, 1, 4, 2, 3))
    kv_s = lambda a: jnp.transpose(
        a.reshape(depth, dec_t, SB_HEADS, SB_DIM, dec_b), (0, 4, 1, 2, 3))
    y_sample = jnp.transpose(xs.reshape(dec_t, dec_b, d_model), (1, 0, 2))
    return (xp.reshape(batch, seq, d_model), y_sample,
            st("gla_p"), st("pool_p"), kv_p(kv_stack_p[0]), kv_p(kv_stack_p[1]),
            st("gla_s"), jnp.transpose(st("pool_s"), (0, 2, 1, 3)),
            kv_s(kv_stack_s[0]), kv_s(kv_stack_s[1]))
```

```python
import functools

import jax
import jax.numpy as jnp
from jax import lax
from jax.experimental import pallas as pl
from jax.experimental.pallas import tpu as pltpu

F32 = jnp.float32
BF16 = jnp.bfloat16

GLA_HEADS = 4
GLA_DK = 32
GLA_DV = 64
GLA_KEY = GLA_HEADS * GLA_DK
GLA_VAL = GLA_HEADS * GLA_DV
GATE_RANK = 16
GATE_TAU = 16.0
GLA_CHUNK = 64
POOL_WINDOWS = (2, 4, 8, 16)
POOL_GROUP_DIM = 64
POOL_WIDTH = 256
POOL_BUF = 15
POOL_HALO = 16
SB_HEADS = 8
SB_DIM = 64
SB_WIDTH = SB_HEADS * SB_DIM
SB_TILE = 128
SB_QBLOCK = 4 * SB_TILE
SB_MASKED = -1e30
PAGE_SIZE = 128
LN_EPS = 1e-5

LANES = 128
SUBLANES = 8
VMEM_LIMIT = 56 * 1024 * 1024

GLA_IN = GLA_KEY + GLA_KEY + GLA_VAL + GLA_VAL + LANES
C_GLA = 0
C_U = C_GLA + GLA_IN
C_Q = C_U + POOL_WIDTH
C_K = C_Q + SB_WIDTH
C_V = C_K + SB_WIDTH
C_END = C_V + SB_WIDTH


def _cparams(sem):
    return pltpu.CompilerParams(dimension_semantics=sem, vmem_limit_bytes=VMEM_LIMIT)


def _dot(a, b):
    return jnp.dot(a, b, preferred_element_type=F32)


def _dot_nt(a, b):
    return lax.dot_general(a, b, (((1,), (1,)), ((), ())), preferred_element_type=F32)


def _dot_tn(a, b):
    return lax.dot_general(a, b, (((0,), (0,)), ((), ())), preferred_element_type=F32)


def _split3(x):
    h = x.astype(BF16)
    r = x - h.astype(F32)
    m = r.astype(BF16)
    l = (r - m.astype(F32)).astype(BF16)
    return h, m, l


def _split2(x):
    h = x.astype(BF16)
    l = (x - h.astype(F32)).astype(BF16)
    return h, l


def _shr(x, n):
    return lax.shift_right_logical(x, jnp.int32(n))


def _log_sigmoid(z):
    return jnp.minimum(z, 0.0) - jnp.log(1.0 + jnp.exp(-jnp.abs(z)))


def _silu(x):
    return x / (1.0 + jnp.exp(-x))


def _layer_norm(y, g, b):
    mu = jnp.mean(y, axis=-1, keepdims=True)
    d = y - mu
    var = jnp.mean(d * d, axis=-1, keepdims=True)
    return d * lax.rsqrt(var + LN_EPS) * g + b


def _in_proj_kernel(x_ref, w_ref, *refs, prompt, aliased):
    if aliased:
        refs = refs[2:]
    gla_ref, u_ref, qb_ref, kt_ref, vt_ref, *extra = refs
    xb = x_ref[...].astype(BF16)
    nat = lambda lo, hi: _dot_nt(xb, w_ref[lo:hi, :])
    gla_ref[...] = nat(C_GLA, C_U) if prompt else _dot_nt(w_ref[C_GLA:C_U, :], xb)
    u_ref[...] = nat(C_U, C_Q)
    qb_ref[...] = (nat(C_Q, C_K) * (SB_DIM ** -0.5)).astype(BF16)
    kt = _dot_nt(w_ref[C_K:C_V, :], xb)
    vt = _dot_nt(w_ref[C_V:C_END, :], xb)
    kt_ref[...] = kt
    vt_ref[...] = vt
    if prompt:
        ktb_ref, vtb_ref = extra
        ktb_ref[...] = kt.astype(BF16)
        vtb_ref[...] = vt.astype(BF16)
    else:
        k_ref, v_ref = extra
        k_ref[...] = nat(C_K, C_V)
        v_ref[...] = nat(C_V, C_END)


def _in_proj(x, w_t, layer, depth, kv_stack, *, groups, tm, prompt):
    n, d = x.shape
    glen = n // groups
    per = glen // tm
    row = lambda c: pl.BlockSpec((tm, c), lambda i: (i, 0))
    ft = pl.BlockSpec((None, SB_WIDTH, tm), lambda i: (i // per, 0, i % per))
    ft_shape = lambda dt: jax.ShapeDtypeStruct((groups, SB_WIDTH, glen), dt)
    stack = pl.BlockSpec((None, None, SB_WIDTH, tm), lambda i: (layer, i // per, 0, i % per))
    stack_shape = jax.ShapeDtypeStruct((depth, groups, SB_WIDTH, glen), F32)
    nat_shape = lambda c, dt: jax.ShapeDtypeStruct((n, c), dt)
    out_specs = [row(GLA_IN), row(POOL_WIDTH), row(SB_WIDTH), stack, stack]
    out_shape = [nat_shape(GLA_IN, F32), nat_shape(POOL_WIDTH, F32),
                 nat_shape(SB_WIDTH, BF16), stack_shape, stack_shape]
    if prompt:
        out_specs += [ft, ft]
        out_shape += [ft_shape(BF16), ft_shape(BF16)]
    else:
        out_specs[0] = pl.BlockSpec((None, GLA_IN, tm), lambda i: (i // per, 0, i % per))
        out_shape[0] = jax.ShapeDtypeStruct((groups, GLA_IN, glen), F32)
        out_specs += [row(SB_WIDTH), row(SB_WIDTH)]
        out_shape += [nat_shape(SB_WIDTH, F32), nat_shape(SB_WIDTH, F32)]
    in_specs = [row(d), pl.BlockSpec((None, C_END, d), lambda i: (layer, 0, 0))]
    operands = [x, w_t]
    aliases = {}
    if kv_stack is not None:
        in_specs += [pl.BlockSpec(memory_space=pl.ANY)] * 2
        operands += list(kv_stack)
        aliases = {2: 3, 3: 4}
    return pl.pallas_call(
        functools.partial(_in_proj_kernel, prompt=prompt, aliased=kv_stack is not None),
        grid=(n // tm,),
        in_specs=in_specs,
        out_specs=out_specs,
        out_shape=out_shape,
        input_output_aliases=aliases,
        compiler_params=_cparams(("arbitrary",)),
        name="in_proj",
    )(*operands)


def _gla_kernel(x_ref, s0_ref, wg_ref, bg_ref, gain_ref, tri_ref, hmean_ref,
                o_ref, s_ref, *, nseq, nsub):
    L = GLA_CHUNK
    c = pl.program_id(1)

    @pl.when(c == 0)
    def _():
        s_ref[...] = s0_ref[...]

    lane_k = lax.broadcasted_iota(jnp.int32, (L, GLA_KEY), 1)
    tq = lax.broadcasted_iota(jnp.int32, (GLA_HEADS * L, L), 0)
    ts = lax.broadcasted_iota(jnp.int32, (GLA_HEADS * L, L), 1)
    causal = (tq & (L - 1)) >= ts
    lane_v = lax.broadcasted_iota(jnp.int32, (L, GLA_VAL), 1)
    srow = lax.broadcasted_iota(jnp.int32, (GLA_KEY, GLA_VAL), 0)
    scol = lax.broadcasted_iota(jnp.int32, (GLA_KEY, GLA_VAL), 1)
    sdiag = _shr(srow, 5) == _shr(scol, 6)
    tri = tri_ref[...]
    hmean = hmean_ref[...]

    items = [(i, sub) for i in range(nseq) for sub in range(nsub)]
    dot3 = lambda x, w: (lambda h, m, l: _dot(h, w) + _dot(m, w) + _dot(l, w))(*_split3(x))

    rows_of = [slice(sub * L, (sub + 1) * L) for _, sub in items]
    blks = [x_ref[i, rows, :] for (i, _), rows in zip(items, rows_of)]
    q =[blk[:, 0:GLA_KEY] * (GLA_DK ** -0.5) for blk in blks]
    k = [blk[:, GLA_KEY:2 * GLA_KEY] for blk in blks]
    vb = [blk[:, 2 * GLA_KEY:2 * GLA_KEY + GLA_VAL].astype(BF16) for blk in blks]
    g = [blk[:, 2 * GLA_KEY + GLA_VAL:2 * GLA_KEY + 2 * GLA_VAL] for blk in blks]

    z = [_dot(blk[:, 2 * GLA_KEY + 2 * GLA_VAL:GLA_IN].astype(BF16), wg_ref[...]) + bg_ref[...]
         for blk in blks]
    log_a = [_log_sigmoid(zi) * (1.0 / GATE_TAU) for zi in z]
    b =[(lambda h, m, l: _dot(tri, h) + _dot(tri, m) + _dot(tri, l))(*_split3(la))
         for la in log_a]
    b_last = [bi[L - 1:L, :] for bi in b]
    b_mid = [bi[L // 2 - 1:L // 2, :] for bi in b]

    def stack_heads(qm):
        return jnp.concatenate(
            [jnp.where(_shr(lane_k, 5) == h, qm, 0.0) for h in range(GLA_HEADS)],
            axis=0).astype(BF16)

    scores = [jnp.where(causal,
                        _dot_nt(stack_heads(q[n] * jnp.exp(b[n] - b_mid[n])),
                                (k[n] * jnp.exp(b_mid[n] - b[n])).astype(BF16)), 0.0)
              for n in range(len(items))]
    upd = [jnp.where(sdiag, _dot_tn((k[n] * jnp.exp(b_last[n] - b[n])).astype(BF16), vb[n]), 0.0)
           for n in range(len(items))]
    p = [_dot(scores[n].astype(BF16), vb[n]) for n in range(len(items))]

    o = []
    for n, (i, sub) in enumerate(items):
        if sub == 0:
            s_bd = s_ref[i]
        o.append(_dot((q[n] * jnp.exp(b[n])).astype(BF16), s_bd.astype(BF16)))
        dcol = jnp.transpose(jnp.broadcast_to(jnp.exp(b_last[n]), (GLA_KEY, GLA_KEY)))
        s_bd = jnp.concatenate([dcol, dcol], axis=1) * s_bd + upd[n]
        if sub == nsub - 1:
            s_ref[i] = s_bd
    for n in range(len(items)):
        for h in range(GLA_HEADS):
            o[n] = o[n] + jnp.where(_shr(lane_v, 6) == h, p[n][h * L:(h + 1) * L, :], 0.0)

    d = [o[n] - dot3(o[n], hmean) for n in range(len(items))]
    var = [dot3(dn * dn, hmean) for dn in d]
    for n, (i, sub) in enumerate(items):
        y = d[n] * lax.rsqrt(var[n] + LN_EPS) * gain_ref[...] * _silu(g[n])
        o_ref[i, rows_of[n], :] = y.astype(BF16)


def _gla(x, s0_bd, wg, layer, bg, gain, *, nseq, nsub):
    b, t, _ = x.shape
    L = GLA_CHUNK
    t_blk = nsub * L
    assert t % t_blk == 0 and b % nseq == 0
    nchunk = t // t_blk
    tri = (lax.broadcasted_iota(jnp.int32, (L, L), 0)
           >= lax.broadcasted_iota(jnp.int32, (L, L), 1)).astype(BF16)
    hm = ((lax.broadcasted_iota(jnp.int32, (GLA_VAL, GLA_VAL), 0) // GLA_DV)
          == (lax.broadcasted_iota(jnp.int32, (GLA_VAL, GLA_VAL), 1) // GLA_DV))
    hmean = jnp.where(hm, 1.0 / GLA_DV, 0.0).astype(BF16)
    const = lambda shape: pl.BlockSpec(shape, lambda s, c: (0,) * len(shape))
    return pl.pallas_call(
        functools.partial(_gla_kernel, nseq=nseq, nsub=nsub),
        grid=(b // nseq, nchunk),
        in_specs=[pl.BlockSpec((nseq, t_blk, GLA_IN), lambda s, c: (s, c, 0)),
                  pl.BlockSpec((nseq, GLA_KEY, GLA_VAL), lambda s, c: (s, 0, 0)),
                  pl.BlockSpec((None, LANES, GLA_KEY), lambda s, c: (layer, 0, 0)),
                  const((1, GLA_KEY)), const((1, GLA_VAL)),
                  const((L, L)), const((GLA_VAL, GLA_VAL))],
        out_specs=[pl.BlockSpec((nseq, t_blk, GLA_VAL), lambda s, c: (s, c, 0)),
                   pl.BlockSpec((nseq, GLA_KEY, GLA_VAL), lambda s, c: (s, 0, 0))],
        out_shape=[jax.ShapeDtypeStruct((b, t, GLA_VAL), BF16),
                   jax.ShapeDtypeStruct((b, GLA_KEY, GLA_VAL), F32)],
        compiler_params=_cparams(("arbitrary", "arbitrary")),
        name="gla",
    )(x, s0_bd, wg, bg, gain, tri, hmean)


def _gla_sample_kernel(g_ref, s_ref, wg_ref, bg_ref, gain_ref, o_ref, so_ref,
                       dec_scr, q_scr, y_scr, *, nt):
    B = g_ref.shape[-1]
    v0 = 2 * GLA_KEY
    g0 = v0 + GLA_VAL
    a0 = g0 + GLA_VAL
    for t in range(nt):
        z = _dot(wg_ref[...], g_ref[t, a0:GLA_IN, :].astype(BF16)) + bg_ref[...]
        dec_scr[t] = jnp.exp(_log_sigmoid(z) * (1.0 / GATE_TAU))
        q_scr[t] = g_ref[t, 0:GLA_KEY, :] * (GLA_DK ** -0.5)

    for h in range(GLA_HEADS):
        def body(kk, o, h=h):
            r = h * GLA_DK + kk
            rows = pl.ds(pl.multiple_of(r * GLA_DV, GLA_DV), GLA_DV)
            s = s_ref[rows, :]
            o = list(o)
            for t in range(nt):
                v = g_ref[t, v0 + h * GLA_DV:v0 + (h + 1) * GLA_DV, :]
                s = dec_scr[t, pl.ds(r, 1), :] * s + g_ref[t, pl.ds(GLA_KEY + r, 1), :] * v
                o[t] = o[t] + q_scr[t, pl.ds(r, 1), :] * s
            so_ref[rows, :] = s
            return tuple(o)

        o = lax.fori_loop(0, GLA_DK, body,
                          tuple(jnp.zeros((GLA_DV, B), F32) for _ in range(nt)))
        hs = slice(h * GLA_DV, (h + 1) * GLA_DV)
        for t in range(nt):
            mu = jnp.mean(o[t], axis=0, keepdims=True)
            d = o[t] - mu
            var = jnp.mean(d * d, axis=0, keepdims=True)
            gate = g_ref[t, g0 + h * GLA_DV:g0 + (h + 1) * GLA_DV, :]
            y_scr[t, hs, :] = d * lax.rsqrt(var + LN_EPS) * gain_ref[hs, :] * _silu(gate)
    for t in range(nt):
        o_ref[t] = jnp.transpose(y_scr[t]).astype(BF16)


def _gla_sample(g_t, state_all, wg_t, layer, bg_col, gain_col):
    nt, _, b = g_t.shape
    ns = GLA_KEY * GLA_DV
    return pl.pallas_call(
        functools.partial(_gla_sample_kernel, nt=nt),
        grid=(1,),
        in_specs=[pl.BlockSpec((nt, GLA_IN, b), lambda i: (0, 0, 0)),
                  pl.BlockSpec((None, ns, b), lambda i: (layer, 0, 0)),
                  pl.BlockSpec((None, GLA_KEY, LANES), lambda i: (layer, 0, 0)),
                  pl.BlockSpec((GLA_KEY, 1), lambda i: (0, 0)),
                  pl.BlockSpec((GLA_VAL, 1), lambda i: (0, 0))],
        out_specs=[pl.BlockSpec((nt, b, GLA_VAL), lambda i: (0, 0, 0)),
                   pl.BlockSpec((ns, b), lambda i: (0, 0))],
        out_shape=[jax.ShapeDtypeStruct((nt, b, GLA_VAL), BF16),
                   jax.ShapeDtypeStruct((ns, b), F32)],
        scratch_shapes=[pltpu.VMEM((nt, GLA_KEY, b), F32), pltpu.VMEM((nt, GLA_KEY, b), F32),
                        pltpu.VMEM((nt, GLA_VAL, b), F32)],
        compiler_params=_cparams(("arbitrary",)),
        name="gla_sample",
    )(g_t, state_all, wg_t, bg_col, gain_col)


def _pool_window(x0, sh, grp):
    s2 = x0 + sh(1)
    s4 = s2 + sh(2) + sh(3)
    s8 = s4 + sh(4) + sh(5) + sh(6) + sh(7)
    s16 = s8 + sh(8) + sh(9) + sh(10) + sh(11) + sh(12) + sh(13) + sh(14) + sh(15)
    return jnp.where(grp == 0, s2, jnp.where(grp == 1, s4, jnp.where(grp == 2, s8, s16)))


def _pool_width(grp):
    return jnp.where(grp == 0, POOL_WINDOWS[0],
                     jnp.where(grp == 1, POOL_WINDOWS[1],
                               jnp.where(grp == 2, POOL_WINDOWS[2], POOL_WINDOWS[3])))


def _pool_prompt_kernel(u_ref, w_ref, scale_ref, o_ref, buf_ref, x_scr, *, tt):
    t = pl.program_id(1)
    H = POOL_HALO

    @pl.when(t == 0)
    def _():
        x_scr[0:H, :] = jnp.zeros((H, POOL_WIDTH), F32)

    x_scr[H:H + tt, :] = u_ref[...]
    x0 = x_scr[H:H + tt, :]
    shape = (tt, POOL_WIDTH)
    grp = _shr(lax.broadcasted_iota(jnp.int32, shape, 1), 6)
    win = _pool_window(x0, lambda j: x_scr[H - j:H - j + tt, :], grp)
    pos = t * tt + lax.broadcasted_iota(jnp.int32, shape, 0)
    cnt = jnp.minimum(pos + 1, _pool_width(grp)).astype(F32)
    d = win / cnt - x0
    o_ref[...] = (_dot(d.astype(BF16), w_ref[...]) * scale_ref[...]).astype(BF16)

    @pl.when(t == pl.num_programs(1) - 1)
    def _():
        buf_ref[...] = x_scr[H + tt - POOL_BUF:H + tt, :]

    x_scr[0:H, :] = x_scr[tt:tt + H, :]


def _pool_prompt(u, w_bd, layer, scale, *, tt):
    b, t, w = u.shape
    return pl.pallas_call(
        functools.partial(_pool_prompt_kernel, tt=tt),
        grid=(b, t // tt),
        in_specs=[pl.BlockSpec((None, tt, w), lambda s, i: (s, i, 0)),
                  pl.BlockSpec((None, w, w), lambda s, i: (layer, 0, 0)),
                  pl.BlockSpec((1, w), lambda s, i: (0, 0))],
        out_specs=[pl.BlockSpec((None, tt, w), lambda s, i: (s, i, 0)),
                   pl.BlockSpec((None, POOL_BUF, w), lambda s, i: (s, 0, 0))],
        out_shape=[jax.ShapeDtypeStruct((b, t, w), BF16),
                   jax.ShapeDtypeStruct((b, POOL_BUF, w), F32)],
        scratch_shapes=[pltpu.VMEM((POOL_HALO + tt, w), F32)],
        compiler_params=_cparams(("arbitrary", "arbitrary")),
        name="pool_prompt",
    )(u, w_bd, scale)


def _pool_sample_kernel(u_ref, prev_ref, w_ref, scale_ref, o_ref, buf_ref, *, nt, pos0):
    ext = [prev_ref[i] for i in range(POOL_BUF)] + [u_ref[t] for t in range(nt)]
    shape = ext[0].shape
    grp = _shr(lax.broadcasted_iota(jnp.int32, shape, 1), 6)
    width = _pool_width(grp)
    for t in range(nt):
        x0 = ext[POOL_BUF + t]
        win = _pool_window(x0, lambda j: ext[POOL_BUF + t - j], grp)
        cnt = jnp.minimum(pos0 + t + 1, width).astype(F32)
        d = win / cnt - x0
        o_ref[t] = (_dot(d.astype(BF16), w_ref[...]) * scale_ref[...]).astype(BF16)
    for i in range(POOL_BUF):
        buf_ref[i] = ext[nt + i]


def _pool_sample(u, prev_all, w_bd, layer, scale, *, pos0):
    nt, b, w = u.shape
    return pl.pallas_call(
        functools.partial(_pool_sample_kernel, nt=nt, pos0=pos0),
        grid=(1,),
        in_specs=[pl.BlockSpec((nt, b, w), lambda i: (0, 0, 0)),
                  pl.BlockSpec((None, POOL_BUF, b, w), lambda i: (layer, 0, 0, 0)),
                  pl.BlockSpec((None, w, w), lambda i: (layer, 0, 0)),
                  pl.BlockSpec((1, w), lambda i: (0, 0))],
        out_specs=[pl.BlockSpec((nt, b, w), lambda i: (0, 0, 0)),
                   pl.BlockSpec((POOL_BUF, b, w), lambda i: (0, 0, 0))],
        out_shape=[jax.ShapeDtypeStruct((nt, b, w), BF16),
                   jax.ShapeDtypeStruct((POOL_BUF, b, w), F32)],
        compiler_params=_cparams(("arbitrary",)),
        name="pool_sample",
    )(u, prev_all, w_bd, scale)


def _softplus(z):
    neg_abs = lax.bitcast_convert_type(
        lax.bitcast_convert_type(z, jnp.uint32) | jnp.uint32(0x80000000), F32)
    return jnp.maximum(z, 0.0) + jnp.log(1.0 + jnp.exp(neg_abs))


def _sb_weights(z_t, hi_t, lo_t, nu, carry, mask):
    return _sb_weights_from(z_t, _sb_suffix(hi_t, lo_t, nu), carry, mask)


def _sb_suffix(hi_t, lo_t, nu):
    return _dot(jnp.concatenate([hi_t, lo_t], axis=1), nu)


def _sb_weights_from(z_t, er, carry, mask):
    x = z_t + er[:, :SB_TILE] + carry
    if mask is not None:
        x = jnp.where(mask, x, SB_MASKED)
    return jnp.exp(x).astype(BF16), carry + er[:, SB_TILE:]


def _sb_nu():
    r = lax.broadcasted_iota(jnp.int32, (2 * SB_TILE, 2 * SB_TILE), 0) & (SB_TILE - 1)
    c = lax.broadcasted_iota(jnp.int32, (2 * SB_TILE, 2 * SB_TILE), 1)
    return jnp.where((c >= SB_TILE) | (r >= c), -1.0, 0.0).astype(BF16)


def _sb_prompt_kernel(bias_ref, q_ref, k_ref, v_ref, nu_ref, o_ref, carry_ref, acc_ref):
    pair = pl.program_id(1)
    qi = pl.program_id(2)
    T = SB_TILE
    TQ = SB_QBLOCK
    q = q_ref[...]
    nu = nu_ref[...]
    col = lax.broadcasted_iota(jnp.int32, (LANES, 4 * T), 1)
    brow = lax.broadcasted_iota(jnp.int32, (LANES, 4 * T), 0)
    bias_full = jnp.where((_shr(col, 7) & 1) == 0, bias_ref[2 * pair], bias_ref[2 * pair + 1])
    b_h, b_m, b_l = [p.astype(F32) for p in _split3(bias_full)]
    bias_rows = jnp.where(brow == 0, b_h, jnp.where(brow == 1, b_m,
                                                    jnp.where(brow == 2, b_l, 0.0))).astype(BF16)
    qlane = lax.broadcasted_iota(jnp.int32, (TQ, LANES), 1)
    q = jnp.concatenate([q, jnp.where(qlane < 3, 1.0, 0.0).astype(BF16)], axis=1)
    head0 = lax.broadcasted_iota(jnp.int32, (LANES, T), 0) < SB_DIM
    colq = lax.broadcasted_iota(jnp.int32, (TQ, 4 * T), 1)
    key_local = (colq & (T - 1)) + jnp.where(colq < 2 * T, T, 0)
    rowq = lax.broadcasted_iota(jnp.int32, (TQ, 4 * T), 0)
    ndiag = TQ // (2 * T)

    def per_head(x):
        zero = jnp.zeros_like(x)
        return [jnp.where(head0, x, zero), jnp.where(head0, zero, x)]

    def run(blocks):
        zs, v_rhss, diags = [], [], []
        for kb, diag_index in blocks:
            start = pl.multiple_of(kb * 2 * T, 2 * T)
            kt = k_ref[:, pl.ds(start, 2 * T)]
            vt = v_ref[:, pl.ds(start, 2 * T)]
            k_rhs = jnp.concatenate(per_head(kt[:, T:]) + per_head(kt[:, :T]), axis=1)
            v_rhss.append(jnp.concatenate(per_head(vt[:, T:]) + per_head(vt[:, :T]), axis=1))
            zs.append(_dot(q, jnp.concatenate([k_rhs, bias_rows], axis=0)))
            diags.append(None if diag_index is None
                         else key_local + diag_index * 2 * T < rowq)
        tiles = [slice(c * T, (c + 1) * T) for c in range(4)]
        ers = []
        for z, diag in zip(zs, diags):
            sp = _softplus(z)
            if diag is not None:
                sp = jnp.where(diag, sp, 0.0)
            hi, lo = _split2(sp)
            ers.append([_sb_suffix(hi[:, sl], lo[:, sl], nu) for sl in tiles])
        carries = [carry_ref[0], carry_ref[1]]
        acc = acc_ref[...]
        for z, diag, er, v_rhs in zip(zs, diags, ers, v_rhss):
            weights = []
            for c, sl in enumerate(tiles):
                a, carries[c % 2] = _sb_weights_from(
                    z[:, sl], er[c], carries[c % 2], None if diag is None else diag[:, sl])
                weights.append(a)
            acc = acc + _dot_nt(jnp.concatenate(weights, axis=1), v_rhs)
        carry_ref[0] = carries[0]
        carry_ref[1] = carries[1]
        acc_ref[...] = acc

    carry_ref[...] = jnp.zeros_like(carry_ref)
    acc_ref[...] = jnp.zeros_like(acc_ref)
    run([(qi * ndiag + d, d) for d in reversed(range(ndiag))])

    def body(s, c):
        first = qi * ndiag - 1 - s * ndiag
        run([(first - u, None) for u in range(ndiag)])
        return c

    lax.fori_loop(0, qi, body, 0)
    o_ref[...] = acc_ref[...].astype(BF16)


def _sb_prompt(qb, ktb, vtb, bias, seq):
    T = SB_TILE
    TQ = SB_QBLOCK
    batch = ktb.shape[0]
    nq = seq // TQ
    return pl.pallas_call(
        _sb_prompt_kernel,
        grid=(batch, SB_HEADS // 2, nq),
        in_specs=[pl.BlockSpec(memory_space=pltpu.SMEM),
                  pl.BlockSpec((TQ, LANES), lambda b, p, i: (b * nq + i, p)),
                  pl.BlockSpec((None, LANES, seq), lambda b, p, i: (b, p, 0)),
                  pl.BlockSpec((None, LANES, seq), lambda b, p, i: (b, p, 0)),
                  pl.BlockSpec((2 * T, 2 * T), lambda b, p, i: (0, 0))],
        out_specs=pl.BlockSpec((TQ, LANES), lambda b, p, i: (b * nq + i, p)),
        out_shape=jax.ShapeDtypeStruct((batch * seq, SB_WIDTH), BF16),
        scratch_shapes=[pltpu.VMEM((2, TQ, T), F32), pltpu.VMEM((TQ, LANES), F32)],
        compiler_params=_cparams(("arbitrary", "arbitrary", "arbitrary")),
        name="sb_prompt",
    )(bias, qb, ktb, vtb, _sb_nu())


def _sb_sample_kernel(pt_ref, q_ref, kn_ref, vn_ref, bias_ref, nu_ref, *rest, npage, tq):
    k_refs = rest[:npage]
    v_refs = rest[npage:2 * npage]
    o_ref = rest[2 * npage]
    pad_ref = rest[2 * npage + 1]
    T = SB_TILE
    R = SB_HEADS * tq
    q = q_ref[0]
    lane = lax.broadcasted_iota(jnp.int32, (tq, SB_WIDTH), 1)
    qbd = jnp.concatenate(
        [jnp.where(_shr(lane, 6) == h, q, 0.0) for h in range(SB_HEADS)],
        axis=0).astype(BF16)
    bias = bias_ref[...]
    nu = nu_ref[...]

    def padded(ref):
        pad_ref[...] = jnp.zeros((T, SB_WIDTH), F32)
        pad_ref[0:tq, :] = ref[0]
        return pad_ref[...].astype(BF16)

    kn = padded(kn_ref)
    vn = padded(vn_ref)
    kt = jnp.concatenate([r[...].astype(BF16) for r in k_refs], axis=1)
    vt = jnp.concatenate([r[...].astype(BF16) for r in v_refs], axis=1)
    z_all = jnp.concatenate([_dot_nt(qbd, kn), _dot(qbd, kt)], axis=1)
    key = lax.broadcasted_iota(jnp.int32, (R, T), 1)
    qt = lax.broadcasted_iota(jnp.int32, (R, T), 0) & (tq - 1)
    new_mask = key < qt

    carry = jnp.zeros((R, T), F32)
    weights = []
    for t in range(npage + 1):
        z = z_all[:, t * T:(t + 1) * T] + bias
        sp = _softplus(z)
        mask = new_mask if t == 0 else None
        if mask is not None:
            sp = jnp.where(mask, sp, 0.0)
        hi, lo = _split2(sp)
        a, carry = _sb_weights(z, hi, lo, nu, carry, mask)
        weights.append(a)
    acc = _dot(weights[0], vn) + _dot_nt(jnp.concatenate(weights[1:], axis=1), vt)

    out = jnp.zeros((tq, SB_WIDTH), F32)
    for h in range(SB_HEADS):
        out = out + jnp.where(_shr(lane, 6) == h, acc[h * tq:(h + 1) * tq, :], 0.0)
    o_ref[0] = out.astype(BF16)


def _sb_sample(q8, kn8, vn8, bias, cache_kt, cache_vt, page_table, layer):
    b, tq, _ = q8.shape
    npage = page_table.shape[1]
    R = SB_HEADS * tq
    bias_tile = jnp.broadcast_to(jnp.repeat(bias.astype(F32), tq)[:, None], (R, SB_TILE))
    tok = pl.BlockSpec((1, tq, SB_WIDTH), lambda s, pt: (s, 0, 0))

    def page_spec(i):
        return pl.BlockSpec((None, None, SB_WIDTH, PAGE_SIZE),
                            lambda s, pt: (layer, pt[s, npage - 1 - i], 0, 0))

    grid_spec = pltpu.PrefetchScalarGridSpec(
        num_scalar_prefetch=1,
        grid=(b,),
        in_specs=[tok, tok, tok,
                  pl.BlockSpec((R, SB_TILE), lambda s, pt: (0, 0)),
                  pl.BlockSpec((2 * SB_TILE, 2 * SB_TILE), lambda s, pt: (0, 0))]
                 + [page_spec(i) for i in range(npage)]
                 + [page_spec(i) for i in range(npage)],
        out_specs=pl.BlockSpec((1, tq, SB_WIDTH), lambda s, pt: (s, 0, 0)),
        scratch_shapes=[pltpu.VMEM((SB_TILE, SB_WIDTH), F32)],
    )
    return pl.pallas_call(
        functools.partial(_sb_sample_kernel, npage=npage, tq=tq),
        grid_spec=grid_spec,
        out_shape=jax.ShapeDtypeStruct((b, tq, SB_WIDTH), BF16),
        compiler_params=_cparams(("arbitrary",)),
        name="sb_sample",
    )(page_table, q8, kn8, vn8, bias_tile, _sb_nu(),
      *([cache_kt] * npage), *([cache_vt] * npage))


def _out_proj_kernel(og_ref, op_ref, os_ref, x_ref, w_ref, g_ref, b_ref, o_ref, *, alpha):
    mix = (_dot(og_ref[...], w_ref[0:GLA_VAL, :])
           + _dot(op_ref[...], w_ref[GLA_VAL:GLA_VAL + POOL_WIDTH, :])
           + _dot(os_ref[...], w_ref[GLA_VAL + POOL_WIDTH:, :]))
    o_ref[...] = _layer_norm(alpha * x_ref[...] + mix, g_ref[...], b_ref[...])


def _out_proj(og, op, os_, x, w, layer, g, b, alpha):
    n, d = x.shape
    tm = min(n, 512)
    row = lambda c: pl.BlockSpec((tm, c), lambda i: (i, 0))
    const = lambda r, c: pl.BlockSpec((r, c), lambda i: (0, 0))
    return pl.pallas_call(
        functools.partial(_out_proj_kernel, alpha=alpha),
        grid=(n // tm,),
        in_specs=[row(GLA_VAL), row(POOL_WIDTH), row(SB_WIDTH), row(d),
                  pl.BlockSpec((None, w.shape[1], d), lambda i: (layer, 0, 0)),
                  const(1, d), const(1, d)],
        out_specs=row(d),
        out_shape=jax.ShapeDtypeStruct((n, d), F32),
        compiler_params=_cparams(("arbitrary",)),
        name="out_proj_ln",
    )(og, op, os_, x, w, g, b)


def _ffn_kernel(x_ref, wg_ref, wu_ref, wo_ref, g_ref, b_ref, o_ref, acc_ref, xb_ref, *, alpha):
    f = pl.program_id(1)

    @pl.when(f == 0)
    def _():
        acc_ref[...] = jnp.zeros_like(acc_ref)
        xb_ref[...] = x_ref[...].astype(BF16)

    xb = xb_ref[...]
    gate = _dot(xb, wg_ref[...])
    up = _dot(xb, wu_ref[...])
    acc_ref[...] += _dot((_silu(gate) * up).astype(BF16), wo_ref[...])

    @pl.when(f == pl.num_programs(1) - 1)
    def _():
        o_ref[...] = _layer_norm(alpha * x_ref[...] + acc_ref[...], g_ref[...], b_ref[...])


def _ffn(x, w_in, w_out, layer, g, b, alpha, *, tm, tf):
    n, d = x.shape
    hidden = w_out.shape[1]
    nf = hidden // tf
    return pl.pallas_call(
        functools.partial(_ffn_kernel, alpha=alpha),
        grid=(n // tm, nf),
        in_specs=[pl.BlockSpec((tm, d), lambda i, f: (i, 0)),
                  pl.BlockSpec((None, d, tf), lambda i, f: (layer, 0, f)),
                  pl.BlockSpec((None, d, tf), lambda i, f: (layer, 0, nf + f)),
                  pl.BlockSpec((None, tf, d), lambda i, f: (layer, f, 0)),
                  pl.BlockSpec((1, d), lambda i, f: (0, 0)),
                  pl.BlockSpec((1, d), lambda i, f: (0, 0))],
        out_specs=pl.BlockSpec((tm, d), lambda i, f: (i, 0)),
        out_shape=jax.ShapeDtypeStruct((n, d), F32),
        scratch_shapes=[pltpu.VMEM((tm, d), F32), pltpu.VMEM((tm, d), BF16)],
        compiler_params=_cparams(("arbitrary", "arbitrary")),
        name="ffn_ln",
    )(x, w_in, w_in, w_out, g, b)


def _gla_state_from_bd(bd):
    b = bd.shape[0]
    s = bd.reshape(b, GLA_HEADS, GLA_DK, GLA_HEADS, GLA_DV)
    return jnp.stack([s[:, h, :, h, :] for h in range(GLA_HEADS)], axis=1)


def kernel(x_prompt, x_sample, state_gla, state_pool, cache_k, cache_v, page_table,
           w_in, w_gate_up, b_gate, gla_norm_g, w_pool, pool_scale, sb_bias, w_out,
           ln1_g, ln1_b, w_ffn_in, w_ffn_out, ln2_g, ln2_b):
    depth = w_in.shape[0]
    batch, seq, d_model = x_prompt.shape
    dec_b, dec_t, _ = x_sample.shape
    n_phys = cache_k.shape[1]
    past_len = page_table.shape[1] * PAGE_SIZE
    alpha = (2.0 * depth) ** 0.25
    assert cache_k.shape[2] == PAGE_SIZE and seq % 512 == 0
    assert dec_t <= SUBLANES and dec_b % LANES == 0
    t_pad = SUBLANES

    w_in_t = jnp.transpose(w_in, (0, 2, 1))
    o = 0
    rows = {}
    for name, size in (("q", GLA_KEY), ("k", GLA_KEY), ("v", GLA_VAL), ("a", GATE_RANK),
                       ("g", GLA_VAL), ("u", POOL_WIDTH), ("qs", SB_WIDTH),
                       ("ks", SB_WIDTH), ("vs", SB_WIDTH)):
        rows[name] = w_in_t[:, o:o + size, :]
        o += size
    a_pad = jnp.zeros((depth, LANES - GATE_RANK, d_model), w_in.dtype)
    w_in_p = jnp.concatenate(
        [rows["q"], rows["k"], rows["v"], rows["g"], rows["a"], a_pad,
         rows["u"], rows["qs"], rows["ks"], rows["vs"]], axis=1).astype(BF16)
    wg_p = jnp.pad(w_gate_up, ((0, 0), (0, LANES - GATE_RANK), (0, 0))).astype(BF16)
    wg_t = jnp.transpose(wg_p, (0, 2, 1))
    gla_state_t = jnp.transpose(state_gla, (0, 2, 3, 4, 1)).reshape(
        depth, GLA_KEY * GLA_DV, dec_b)
    ngrp = len(POOL_WINDOWS)
    eye_g = jnp.eye(ngrp, dtype=bool)[None, :, None, :, None]
    w_pool_bd = jnp.where(eye_g, w_pool[:, :, :, None, :], 0.0).reshape(
        depth, POOL_WIDTH, POOL_WIDTH).astype(BF16)
    w_out_b = w_out.astype(BF16)
    w_ffn_in_b = w_ffn_in.astype(BF16)
    w_ffn_out_b = w_ffn_out.astype(BF16)
    cache_kt = jnp.transpose(cache_k, (0, 1, 3, 4, 2)).reshape(depth, n_phys, SB_WIDTH, PAGE_SIZE)
    cache_vt = jnp.transpose(cache_v, (0, 1, 3, 4, 2)).reshape(depth, n_phys, SB_WIDTH, PAGE_SIZE)
    pool_prev = jnp.transpose(state_pool, (0, 2, 1, 3))
    row2 = lambda a: a.reshape(1, -1).astype(F32)

    xp = x_prompt.reshape(batch * seq, d_model)
    xs = jnp.transpose(x_sample, (1, 0, 2)).reshape(dec_t * dec_b, d_model)
    zeros_state = jnp.zeros((batch, GLA_KEY, GLA_VAL), F32)
    outs = {k: [] for k in ("gla_p", "pool_p", "gla_s", "pool_s")}
    kv_stack_p = kv_stack_s = None

    def seq_major(a):
        a = jnp.transpose(a.reshape(dec_t, dec_b, a.shape[-1]), (1, 0, 2))
        return jnp.pad(a, ((0, 0), (0, t_pad - dec_t), (0, 0)))

    def tok_major(a):
        return jnp.transpose(a[:, :dec_t, :], (1, 0, 2)).reshape(dec_t * dec_b, a.shape[-1])

    for l in range(depth):
        bg, gain = row2(b_gate[l]), row2(gla_norm_g[l])
        pscale = row2(pool_scale[l])
        g1, b1, g2, b2 = row2(ln1_g[l]), row2(ln1_b[l]), row2(ln2_g[l]), row2(ln2_b[l])

        gla_in, u, qb, *kv_stack_p, ktb, vtb = _in_proj(xp, w_in_p, l, depth, kv_stack_p,
                                                        groups=batch, tm=512, prompt=True)
        o_gla, s_bd = _gla(gla_in.reshape(batch, seq, GLA_IN), zeros_state, wg_p, l, bg, gain,
                           nseq=batch, nsub=4)
        o_pool, pbuf = _pool_prompt(u.reshape(batch, seq, POOL_WIDTH), w_pool_bd, l, pscale,
                                    tt=512)
        o_sb = _sb_prompt(qb, ktb, vtb, sb_bias[l].astype(F32), seq)
        xp = _out_proj(o_gla.reshape(batch * seq, GLA_VAL), o_pool.reshape(batch * seq, POOL_WIDTH),
                       o_sb, xp, w_out_b, l, g1, b1, alpha)
        xp = _ffn(xp, w_ffn_in_b, w_ffn_out_b, l, g2, b2, alpha, tm=1024, tf=256)
        outs["gla_p"].append(_gla_state_from_bd(s_bd))
        outs["pool_p"].append(pbuf)

        gla_in, u, qb, *kv_stack_s, k, v = _in_proj(xs, w_in_p, l, depth, kv_stack_s,
                                                    groups=dec_t, tm=dec_b, prompt=False)
        o_gla, s_new = _gla_sample(gla_in, gla_state_t, wg_t, l, b_gate[l].reshape(-1, 1),
                                   gla_norm_g[l].reshape(-1, 1))
        o_pool, pbuf = _pool_sample(u.reshape(dec_t, dec_b, POOL_WIDTH), pool_prev, w_pool_bd, l,
                                    pscale, pos0=past_len)
        o_sb = _sb_sample(seq_major(qb.astype(F32)), seq_major(k), seq_major(v), sb_bias[l],
                          cache_kt, cache_vt, page_table, l)
        xs = _out_proj(o_gla.reshape(dec_t * dec_b, GLA_VAL),
                       o_pool.reshape(dec_t * dec_b, POOL_WIDTH),
                       tok_major(o_sb), xs, w_out_b, l, g1, b1, alpha)
        xs = _ffn(xs, w_ffn_in_b, w_ffn_out_b, l, g2, b2, alpha, tm=512, tf=256)
        outs["gla_s"].append(s_new)
        outs["pool_s"].append(pbuf)

    st = lambda key: jnp.stack(outs[key])
    kv_p = lambda a: jnp.transpose(
        a.reshape(depth, batch, SB_HEADS, SB_DIM, seq), (0, 1, 4, 2, 3))
    kv_s = lambda a: jnp.transpose(
        a.reshape(depth, dec_t, SB_HEADS, SB_DIM, dec_b), (0, 4, 1, 2, 3))
    y_sample = jnp.transpose(xs.reshape(dec_t, dec_b, d_model), (1, 0, 2))
    return (xp.reshape(batch, seq, d_model), y_sample,
            st("gla_p"), st("pool_p"), kv_p(kv_stack_p[0]), kv_p(kv_stack_p[1]),
            jnp.transpose(st("gla_s").reshape(depth, GLA_HEADS, GLA_DK, GLA_DV, dec_b),
                          (0, 4, 1, 2, 3)),
            jnp.transpose(st("pool_s"), (0, 2, 1, 3)),
            kv_s(kv_stack_s[0]), kv_s(kv_stack_s[1]))
```

```python
import functools

import jax
import jax.numpy as jnp
from jax import lax
from jax.experimental import pallas as pl
from jax.experimental.pallas import tpu as pltpu

F32 = jnp.float32
BF16 = jnp.bfloat16

GLA_HEADS = 4
GLA_DK = 32
GLA_DV = 64
GLA_KEY = GLA_HEADS * GLA_DK
GLA_VAL = GLA_HEADS * GLA_DV
GATE_RANK = 16
GATE_TAU = 16.0
GLA_CHUNK = 64
POOL_WINDOWS = (2, 4, 8, 16)
POOL_GROUP_DIM = 64
POOL_WIDTH = 256
POOL_BUF = 15
POOL_HALO = 16
SB_HEADS = 8
SB_DIM = 64
SB_WIDTH = SB_HEADS * SB_DIM
SB_TILE = 128
SB_QBLOCK = 4 * SB_TILE
SB_MASKED = -1e30
PAGE_SIZE = 128
LN_EPS = 1e-5

LANES = 128
SUBLANES = 8
VMEM_LIMIT = 56 * 1024 * 1024

GLA_IN = GLA_KEY + GLA_KEY + GLA_VAL + GLA_VAL + LANES
C_GLA = 0
C_U = C_GLA + GLA_IN
C_Q = C_U + POOL_WIDTH
C_K = C_Q + SB_WIDTH
C_V = C_K + SB_WIDTH
C_END = C_V + SB_WIDTH


def _cparams(sem):
    return pltpu.CompilerParams(dimension_semantics=sem, vmem_limit_bytes=VMEM_LIMIT)


def _dot(a, b):
    return jnp.dot(a, b, preferred_element_type=F32)


def _dot_nt(a, b):
    return lax.dot_general(a, b, (((1,), (1,)), ((), ())), preferred_element_type=F32)


def _dot_tn(a, b):
    return lax.dot_general(a, b, (((0,), (0,)), ((), ())), preferred_element_type=F32)


def _split3(x):
    h = x.astype(BF16)
    r = x - h.astype(F32)
    m = r.astype(BF16)
    l = (r - m.astype(F32)).astype(BF16)
    return h, m, l


def _split2(x):
    h = x.astype(BF16)
    l = (x - h.astype(F32)).astype(BF16)
    return h, l


def _shr(x, n):
    return lax.shift_right_logical(x, jnp.int32(n))


def _log_sigmoid(z):
    return jnp.minimum(z, 0.0) - jnp.log(1.0 + jnp.exp(-jnp.abs(z)))


def _silu(x):
    return x / (1.0 + jnp.exp(-x))


def _layer_norm(y, g, b):
    mu = jnp.mean(y, axis=-1, keepdims=True)
    d = y - mu
    var = jnp.mean(d * d, axis=-1, keepdims=True)
    return d * lax.rsqrt(var + LN_EPS) * g + b


def _in_proj_kernel(x_ref, w_ref, *refs, prompt, aliased):
    if aliased:
        refs = refs[2:]
    gla_ref, u_ref, qb_ref, kt_ref, vt_ref, *extra = refs
    xb = x_ref[...].astype(BF16)
    nat = lambda lo, hi: _dot_nt(xb, w_ref[lo:hi, :])
    gla_ref[...] = nat(C_GLA, C_U) if prompt else _dot_nt(w_ref[C_GLA:C_U, :], xb)
    u_ref[...] = nat(C_U, C_Q)
    qb_ref[...] = (nat(C_Q, C_K) * (SB_DIM ** -0.5)).astype(BF16)
    kt = _dot_nt(w_ref[C_K:C_V, :], xb)
    vt = _dot_nt(w_ref[C_V:C_END, :], xb)
    if aliased:
        kt_ref[...] = kt
        vt_ref[...] = vt
    else:
        for ref, val in ((kt_ref, kt), (vt_ref, vt)):
            ref[0] = val
            ref[1:] = jnp.zeros((ref.shape[0] - 1,) + val.shape, F32)
    if prompt:
        ktb_ref, vtb_ref = extra
        ktb_ref[...] = kt.astype(BF16)
        vtb_ref[...] = vt.astype(BF16)
    else:
        k_ref, v_ref = extra
        k_ref[...] = nat(C_K, C_V)
        v_ref[...] = nat(C_V, C_END)


def _in_proj(x, w_t, layer, depth, kv_stack, *, groups, tm, prompt):
    n, d = x.shape
    glen = n // groups
    per = glen // tm
    row = lambda c: pl.BlockSpec((tm, c), lambda i: (i, 0))
    ft = pl.BlockSpec((None, SB_WIDTH, tm), lambda i: (i // per, 0, i % per))
    ft_shape = lambda dt: jax.ShapeDtypeStruct((groups, SB_WIDTH, glen), dt)
    if kv_stack is None:
        stack = pl.BlockSpec((depth, None, SB_WIDTH, tm), lambda i: (0, i // per, 0, i % per))
    else:
        stack = pl.BlockSpec((None, None, SB_WIDTH, tm),
                             lambda i: (layer, i // per, 0, i % per))
    stack_shape = jax.ShapeDtypeStruct((depth, groups, SB_WIDTH, glen), F32)
    nat_shape = lambda c, dt: jax.ShapeDtypeStruct((n, c), dt)
    out_specs = [row(GLA_IN), row(POOL_WIDTH), row(SB_WIDTH), stack, stack]
    out_shape = [nat_shape(GLA_IN, F32), nat_shape(POOL_WIDTH, F32),
                 nat_shape(SB_WIDTH, BF16), stack_shape, stack_shape]
    if prompt:
        out_specs += [ft, ft]
        out_shape += [ft_shape(BF16), ft_shape(BF16)]
    else:
        out_specs[0] = pl.BlockSpec((None, GLA_IN, tm), lambda i: (i // per, 0, i % per))
        out_shape[0] = jax.ShapeDtypeStruct((groups, GLA_IN, glen), F32)
        out_specs += [row(SB_WIDTH), row(SB_WIDTH)]
        out_shape += [nat_shape(SB_WIDTH, F32), nat_shape(SB_WIDTH, F32)]
    in_specs = [row(d), pl.BlockSpec((None, C_END, d), lambda i: (layer, 0, 0))]
    operands = [x, w_t]
    aliases = {}
    if kv_stack is not None:
        in_specs += [pl.BlockSpec(memory_space=pl.ANY)] * 2
        operands += list(kv_stack)
        aliases = {2: 3, 3: 4}
    return pl.pallas_call(
        functools.partial(_in_proj_kernel, prompt=prompt, aliased=kv_stack is not None),
        grid=(n // tm,),
        in_specs=in_specs,
        out_specs=out_specs,
        out_shape=out_shape,
        input_output_aliases=aliases,
        compiler_params=_cparams(("arbitrary",)),
        name="in_proj",
    )(*operands)


def _gla_kernel(x_ref, s0_ref, wg_ref, bg_ref, gain_ref, tri_ref, hmean_ref,
                o_ref, s_ref, *, nseq, nsub):
    L = GLA_CHUNK
    c = pl.program_id(1)

    @pl.when(c == 0)
    def _():
        s_ref[...] = s0_ref[...]

    lane_k = lax.broadcasted_iota(jnp.int32, (L, GLA_KEY), 1)
    tq = lax.broadcasted_iota(jnp.int32, (GLA_HEADS * L, L), 0)
    ts = lax.broadcasted_iota(jnp.int32, (GLA_HEADS * L, L), 1)
    causal = (tq & (L - 1)) >= ts
    lane_v = lax.broadcasted_iota(jnp.int32, (L, GLA_VAL), 1)
    srow = lax.broadcasted_iota(jnp.int32, (GLA_KEY, GLA_VAL), 0)
    scol = lax.broadcasted_iota(jnp.int32, (GLA_KEY, GLA_VAL), 1)
    sdiag = _shr(srow, 5) == _shr(scol, 6)
    tri = tri_ref[...]
    hmean = hmean_ref[...]

    items = [(i, sub) for i in range(nseq) for sub in range(nsub)]
    dot3 = lambda x, w: (lambda h, m, l: _dot(h, w) + _dot(m, w) + _dot(l, w))(*_split3(x))

    rows_of = [slice(sub * L, (sub + 1) * L) for _, sub in items]
    blks = [x_ref[i, rows, :] for (i, _), rows in zip(items, rows_of)]
    q =[blk[:, 0:GLA_KEY] * (GLA_DK ** -0.5) for blk in blks]
    k = [blk[:, GLA_KEY:2 * GLA_KEY] for blk in blks]
    vb = [blk[:, 2 * GLA_KEY:2 * GLA_KEY + GLA_VAL].astype(BF16) for blk in blks]
    g = [blk[:, 2 * GLA_KEY + GLA_VAL:2 * GLA_KEY + 2 * GLA_VAL] for blk in blks]

    z = [_dot(blk[:, 2 * GLA_KEY + 2 * GLA_VAL:GLA_IN].astype(BF16), wg_ref[...]) + bg_ref[...]
         for blk in blks]
    log_a = [_log_sigmoid(zi) * (1.0 / GATE_TAU) for zi in z]
    b =[(lambda h, m, l: _dot(tri, h) + _dot(tri, m) + _dot(tri, l))(*_split3(la))
         for la in log_a]
    b_last = [bi[L - 1:L, :] for bi in b]
    b_mid = [bi[L // 2 - 1:L // 2, :] for bi in b]

    def stack_heads(qm):
        return jnp.concatenate(
            [jnp.where(_shr(lane_k, 5) == h, qm, 0.0) for h in range(GLA_HEADS)],
            axis=0).astype(BF16)

    scores = [jnp.where(causal,
                        _dot_nt(stack_heads(q[n] * jnp.exp(b[n] - b_mid[n])),
                                (k[n] * jnp.exp(b_mid[n] - b[n])).astype(BF16)), 0.0)
              for n in range(len(items))]
    upd = [jnp.where(sdiag, _dot_tn((k[n] * jnp.exp(b_last[n] - b[n])).astype(BF16), vb[n]), 0.0)
           for n in range(len(items))]
    p = [_dot(scores[n].astype(BF16), vb[n]) for n in range(len(items))]

    o = []
    for n, (i, sub) in enumerate(items):
        if sub == 0:
            s_bd = s_ref[i]
        o.append(_dot((q[n] * jnp.exp(b[n])).astype(BF16), s_bd.astype(BF16)))
        dcol = jnp.transpose(jnp.broadcast_to(jnp.exp(b_last[n]), (GLA_KEY, GLA_KEY)))
        s_bd = jnp.concatenate([dcol, dcol], axis=1) * s_bd + upd[n]
        if sub == nsub - 1:
            s_ref[i] = s_bd
    for n in range(len(items)):
        for h in range(GLA_HEADS):
            o[n] = o[n] + jnp.where(_shr(lane_v, 6) == h, p[n][h * L:(h + 1) * L, :], 0.0)

    d = [o[n] - dot3(o[n], hmean) for n in range(len(items))]
    var = [dot3(dn * dn, hmean) for dn in d]
    for n, (i, sub) in enumerate(items):
        y = d[n] * lax.rsqrt(var[n] + LN_EPS) * gain_ref[...] * _silu(g[n])
        o_ref[i, rows_of[n], :] = y.astype(BF16)


def _gla(x, s0_bd, wg, layer, bg, gain, *, nseq, nsub):
    b, t, _ = x.shape
    L = GLA_CHUNK
    t_blk = nsub * L
    assert t % t_blk == 0 and b % nseq == 0
    nchunk = t // t_blk
    tri = (lax.broadcasted_iota(jnp.int32, (L, L), 0)
           >= lax.broadcasted_iota(jnp.int32, (L, L), 1)).astype(BF16)
    hm = ((lax.broadcasted_iota(jnp.int32, (GLA_VAL, GLA_VAL), 0) // GLA_DV)
          == (lax.broadcasted_iota(jnp.int32, (GLA_VAL, GLA_VAL), 1) // GLA_DV))
    hmean = jnp.where(hm, 1.0 / GLA_DV, 0.0).astype(BF16)
    const = lambda shape: pl.BlockSpec(shape, lambda s, c: (0,) * len(shape))
    return pl.pallas_call(
        functools.partial(_gla_kernel, nseq=nseq, nsub=nsub),
        grid=(b // nseq, nchunk),
        in_specs=[pl.BlockSpec((nseq, t_blk, GLA_IN), lambda s, c: (s, c, 0)),
                  pl.BlockSpec((nseq, GLA_KEY, GLA_VAL), lambda s, c: (s, 0, 0)),
                  pl.BlockSpec((None, LANES, GLA_KEY), lambda s, c: (layer, 0, 0)),
                  const((1, GLA_KEY)), const((1, GLA_VAL)),
                  const((L, L)), const((GLA_VAL, GLA_VAL))],
        out_specs=[pl.BlockSpec((nseq, t_blk, GLA_VAL), lambda s, c: (s, c, 0)),
                   pl.BlockSpec((nseq, GLA_KEY, GLA_VAL), lambda s, c: (s, 0, 0))],
        out_shape=[jax.ShapeDtypeStruct((b, t, GLA_VAL), BF16),
                   jax.ShapeDtypeStruct((b, GLA_KEY, GLA_VAL), F32)],
        compiler_params=_cparams(("arbitrary", "arbitrary")),
        name="gla",
    )(x, s0_bd, wg, bg, gain, tri, hmean)


def _gla_sample_kernel(g_ref, s_ref, wg_ref, bg_ref, gain_ref, o_ref, so_ref,
                       dec_scr, q_scr, y_scr, *, nt):
    B = g_ref.shape[-1]
    v0 = 2 * GLA_KEY
    g0 = v0 + GLA_VAL
    a0 = g0 + GLA_VAL
    for t in range(nt):
        z = _dot(wg_ref[...], g_ref[t, a0:GLA_IN, :].astype(BF16)) + bg_ref[...]
        dec_scr[t] = jnp.exp(_log_sigmoid(z) * (1.0 / GATE_TAU))
        q_scr[t] = g_ref[t, 0:GLA_KEY, :] * (GLA_DK ** -0.5)

    for h in range(GLA_HEADS):
        def body(kk, o, h=h):
            r = h * GLA_DK + kk
            rows = pl.ds(pl.multiple_of(r * GLA_DV, GLA_DV), GLA_DV)
            s = s_ref[rows, :]
            o = list(o)
            for t in range(nt):
                v = g_ref[t, v0 + h * GLA_DV:v0 + (h + 1) * GLA_DV, :]
                s = dec_scr[t, pl.ds(r, 1), :] * s + g_ref[t, pl.ds(GLA_KEY + r, 1), :] * v
                o[t] = o[t] + q_scr[t, pl.ds(r, 1), :] * s
            so_ref[rows, :] = s
            return tuple(o)

        o = lax.fori_loop(0, GLA_DK, body,
                          tuple(jnp.zeros((GLA_DV, B), F32) for _ in range(nt)))
        hs = slice(h * GLA_DV, (h + 1) * GLA_DV)
        for t in range(nt):
            mu = jnp.mean(o[t], axis=0, keepdims=True)
            d = o[t] - mu
            var = jnp.mean(d * d, axis=0, keepdims=True)
            gate = g_ref[t, g0 + h * GLA_DV:g0 + (h + 1) * GLA_DV, :]
            y_scr[t, hs, :] = d * lax.rsqrt(var + LN_EPS) * gain_ref[hs, :] * _silu(gate)
    for t in range(nt):
        o_ref[t] = jnp.transpose(y_scr[t]).astype(BF16)


def _gla_sample(g_t, state_all, wg_t, layer, bg_col, gain_col):
    nt, _, b = g_t.shape
    ns = GLA_KEY * GLA_DV
    return pl.pallas_call(
        functools.partial(_gla_sample_kernel, nt=nt),
        grid=(1,),
        in_specs=[pl.BlockSpec((nt, GLA_IN, b), lambda i: (0, 0, 0)),
                  pl.BlockSpec((None, ns, b), lambda i: (layer, 0, 0)),
                  pl.BlockSpec((None, GLA_KEY, LANES), lambda i: (layer, 0, 0)),
                  pl.BlockSpec((GLA_KEY, 1), lambda i: (0, 0)),
                  pl.BlockSpec((GLA_VAL, 1), lambda i: (0, 0))],
        out_specs=[pl.BlockSpec((nt, b, GLA_VAL), lambda i: (0, 0, 0)),
                   pl.BlockSpec((ns, b), lambda i: (0, 0))],
        out_shape=[jax.ShapeDtypeStruct((nt, b, GLA_VAL), BF16),
                   jax.ShapeDtypeStruct((ns, b), F32)],
        scratch_shapes=[pltpu.VMEM((nt, GLA_KEY, b), F32), pltpu.VMEM((nt, GLA_KEY, b), F32),
                        pltpu.VMEM((nt, GLA_VAL, b), F32)],
        compiler_params=_cparams(("arbitrary",)),
        name="gla_sample",
    )(g_t, state_all, wg_t, bg_col, gain_col)


def _pool_window(x0, sh, grp):
    s2 = x0 + sh(1)
    s4 = s2 + sh(2) + sh(3)
    s8 = s4 + sh(4) + sh(5) + sh(6) + sh(7)
    s16 = s8 + sh(8) + sh(9) + sh(10) + sh(11) + sh(12) + sh(13) + sh(14) + sh(15)
    return jnp.where(grp == 0, s2, jnp.where(grp == 1, s4, jnp.where(grp == 2, s8, s16)))


def _pool_width(grp):
    return jnp.where(grp == 0, POOL_WINDOWS[0],
                     jnp.where(grp == 1, POOL_WINDOWS[1],
                               jnp.where(grp == 2, POOL_WINDOWS[2], POOL_WINDOWS[3])))


def _pool_prompt_kernel(u_ref, w_ref, scale_ref, o_ref, buf_ref, x_scr, *, tt):
    t = pl.program_id(1)
    H = POOL_HALO

    @pl.when(t == 0)
    def _():
        x_scr[0:H, :] = jnp.zeros((H, POOL_WIDTH), F32)

    x_scr[H:H + tt, :] = u_ref[...]
    x0 = x_scr[H:H + tt, :]
    shape = (tt, POOL_WIDTH)
    grp = _shr(lax.broadcasted_iota(jnp.int32, shape, 1), 6)
    win = _pool_window(x0, lambda j: x_scr[H - j:H - j + tt, :], grp)
    pos = t * tt + lax.broadcasted_iota(jnp.int32, shape, 0)
    cnt = jnp.minimum(pos + 1, _pool_width(grp)).astype(F32)
    d = win / cnt - x0
    o_ref[...] = (_dot(d.astype(BF16), w_ref[...]) * scale_ref[...]).astype(BF16)

    @pl.when(t == pl.num_programs(1) - 1)
    def _():
        buf_ref[...] = x_scr[H + tt - POOL_BUF:H + tt, :]

    x_scr[0:H, :] = x_scr[tt:tt + H, :]


def _pool_prompt(u, w_bd, layer, scale, *, tt):
    b, t, w = u.shape
    return pl.pallas_call(
        functools.partial(_pool_prompt_kernel, tt=tt),
        grid=(b, t // tt),
        in_specs=[pl.BlockSpec((None, tt, w), lambda s, i: (s, i, 0)),
                  pl.BlockSpec((None, w, w), lambda s, i: (layer, 0, 0)),
                  pl.BlockSpec((1, w), lambda s, i: (0, 0))],
        out_specs=[pl.BlockSpec((None, tt, w), lambda s, i: (s, i, 0)),
                   pl.BlockSpec((None, POOL_BUF, w), lambda s, i: (s, 0, 0))],
        out_shape=[jax.ShapeDtypeStruct((b, t, w), BF16),
                   jax.ShapeDtypeStruct((b, POOL_BUF, w), F32)],
        scratch_shapes=[pltpu.VMEM((POOL_HALO + tt, w), F32)],
        compiler_params=_cparams(("arbitrary", "arbitrary")),
        name="pool_prompt",
    )(u, w_bd, scale)


def _pool_sample_kernel(u_ref, prev_ref, w_ref, scale_ref, o_ref, buf_ref, *, nt, pos0):
    ext = [prev_ref[i] for i in range(POOL_BUF)] + [u_ref[t] for t in range(nt)]
    shape = ext[0].shape
    grp = _shr(lax.broadcasted_iota(jnp.int32, shape, 1), 6)
    width = _pool_width(grp)
    for t in range(nt):
        x0 = ext[POOL_BUF + t]
        win = _pool_window(x0, lambda j: ext[POOL_BUF + t - j], grp)
        cnt = jnp.minimum(pos0 + t + 1, width).astype(F32)
        d = win / cnt - x0
        o_ref[t] = (_dot(d.astype(BF16), w_ref[...]) * scale_ref[...]).astype(BF16)
    for i in range(POOL_BUF):
        buf_ref[i] = ext[nt + i]


def _pool_sample(u, prev_all, w_bd, layer, scale, *, pos0):
    nt, b, w = u.shape
    return pl.pallas_call(
        functools.partial(_pool_sample_kernel, nt=nt, pos0=pos0),
        grid=(1,),
        in_specs=[pl.BlockSpec((nt, b, w), lambda i: (0, 0, 0)),
                  pl.BlockSpec((None, POOL_BUF, b, w), lambda i: (layer, 0, 0, 0)),
                  pl.BlockSpec((None, w, w), lambda i: (layer, 0, 0)),
                  pl.BlockSpec((1, w), lambda i: (0, 0))],
        out_specs=[pl.BlockSpec((nt, b, w), lambda i: (0, 0, 0)),
                   pl.BlockSpec((POOL_BUF, b, w), lambda i: (0, 0, 0))],
        out_shape=[jax.ShapeDtypeStruct((nt, b, w), BF16),
                   jax.ShapeDtypeStruct((POOL_BUF, b, w), F32)],
        compiler_params=_cparams(("arbitrary",)),
        name="pool_sample",
    )(u, prev_all, w_bd, scale)


def _softplus(z):
    neg_abs = lax.bitcast_convert_type(
        lax.bitcast_convert_type(z, jnp.uint32) | jnp.uint32(0x80000000), F32)
    return jnp.maximum(z, 0.0) + jnp.log(1.0 + jnp.exp(neg_abs))


def _sb_weights(z_t, hi_t, lo_t, nu, carry, mask):
    return _sb_weights_from(z_t, _sb_suffix(hi_t, lo_t, nu), carry, mask)


def _sb_suffix(hi_t, lo_t, nu):
    return _dot(jnp.concatenate([hi_t, lo_t], axis=1), nu)


def _sb_weights_from(z_t, er, carry, mask):
    x = z_t + er[:, :SB_TILE] + carry
    if mask is not None:
        x = jnp.where(mask, x, SB_MASKED)
    return jnp.exp(x).astype(BF16), carry + er[:, SB_TILE:]


def _sb_nu():
    r = lax.broadcasted_iota(jnp.int32, (2 * SB_TILE, 2 * SB_TILE), 0) & (SB_TILE - 1)
    c = lax.broadcasted_iota(jnp.int32, (2 * SB_TILE, 2 * SB_TILE), 1)
    return jnp.where((c >= SB_TILE) | (r >= c), -1.0, 0.0).astype(BF16)


def _sb_prompt_kernel(bias_ref, q_ref, k_ref, v_ref, nu_ref, o_ref, carry_ref, acc_ref):
    pair = pl.program_id(1)
    qi = pl.program_id(2)
    T = SB_TILE
    TQ = SB_QBLOCK
    q = q_ref[...]
    nu = nu_ref[...]
    col = lax.broadcasted_iota(jnp.int32, (LANES, 4 * T), 1)
    brow = lax.broadcasted_iota(jnp.int32, (LANES, 4 * T), 0)
    bias_full = jnp.where((_shr(col, 7) & 1) == 0, bias_ref[2 * pair], bias_ref[2 * pair + 1])
    b_h, b_m, b_l = [p.astype(F32) for p in _split3(bias_full)]
    bias_rows = jnp.where(brow == 0, b_h, jnp.where(brow == 1, b_m,
                                                    jnp.where(brow == 2, b_l, 0.0))).astype(BF16)
    qlane = lax.broadcasted_iota(jnp.int32, (TQ, LANES), 1)
    q = jnp.concatenate([q, jnp.where(qlane < 3, 1.0, 0.0).astype(BF16)], axis=1)
    head0 = lax.broadcasted_iota(jnp.int32, (LANES, T), 0) < SB_DIM
    colq = lax.broadcasted_iota(jnp.int32, (TQ, 4 * T), 1)
    key_local = (colq & (T - 1)) + jnp.where(colq < 2 * T, T, 0)
    rowq = lax.broadcasted_iota(jnp.int32, (TQ, 4 * T), 0)
    ndiag = TQ // (2 * T)

    def per_head(x):
        zero = jnp.zeros_like(x)
        return [jnp.where(head0, x, zero), jnp.where(head0, zero, x)]

    def run(blocks, r0=0):
        rows = slice(r0, TQ)
        zs, v_rhss, diags = [], [], []
        for kb, diag_index in blocks:
            start = pl.multiple_of(kb * 2 * T, 2 * T)
            kt = k_ref[:, pl.ds(start, 2 * T)]
            vt = v_ref[:, pl.ds(start, 2 * T)]
            k_rhs = jnp.concatenate(per_head(kt[:, T:]) + per_head(kt[:, :T]), axis=1)
            v_rhss.append(jnp.concatenate(per_head(vt[:, T:]) + per_head(vt[:, :T]), axis=1))
            zs.append(_dot(q[rows], jnp.concatenate([k_rhs, bias_rows], axis=0)))
            diags.append(None if diag_index is None
                         else (key_local + diag_index * 2 * T < rowq)[rows])
        tiles = [slice(c * T, (c + 1) * T) for c in range(4)]
        ers = []
        for z, diag in zip(zs, diags):
            sp = _softplus(z)
            if diag is not None:
                sp = jnp.where(diag, sp, 0.0)
            hi, lo = _split2(sp)
            ers.append([_sb_suffix(hi[:, sl], lo[:, sl], nu) for sl in tiles])
        carries = [carry_ref[0, rows, :], carry_ref[1, rows, :]]
        acc = acc_ref[rows, :]
        for z, diag, er, v_rhs in zip(zs, diags, ers, v_rhss):
            weights = []
            for c, sl in enumerate(tiles):
                a, carries[c % 2] = _sb_weights_from(
                    z[:, sl], er[c], carries[c % 2], None if diag is None else diag[:, sl])
                weights.append(a)
            acc = acc + _dot_nt(jnp.concatenate(weights, axis=1), v_rhs)
        carry_ref[0, rows, :] = carries[0]
        carry_ref[1, rows, :] = carries[1]
        acc_ref[rows, :] = acc

    carry_ref[...] = jnp.zeros_like(carry_ref)
    acc_ref[...] = jnp.zeros_like(acc_ref)
    for d in reversed(range(ndiag)):
        run([(qi * ndiag + d, d)], r0=d * 2 * T)

    def body(s, c):
        first = qi * ndiag - 1 - s * ndiag
        run([(first - u, None) for u in range(ndiag)])
        return c

    lax.fori_loop(0, qi, body, 0)
    o_ref[...] = acc_ref[...].astype(BF16)


def _sb_prompt(qb, ktb, vtb, bias, seq):
    T = SB_TILE
    TQ = SB_QBLOCK
    batch = ktb.shape[0]
    nq = seq // TQ
    return pl.pallas_call(
        _sb_prompt_kernel,
        grid=(batch, SB_HEADS // 2, nq),
        in_specs=[pl.BlockSpec(memory_space=pltpu.SMEM),
                  pl.BlockSpec((TQ, LANES), lambda b, p, i: (b * nq + i, p)),
                  pl.BlockSpec((None, LANES, seq), lambda b, p, i: (b, p, 0)),
                  pl.BlockSpec((None, LANES, seq), lambda b, p, i: (b, p, 0)),
                  pl.BlockSpec((2 * T, 2 * T), lambda b, p, i: (0, 0))],
        out_specs=pl.BlockSpec((TQ, LANES), lambda b, p, i: (b * nq + i, p)),
        out_shape=jax.ShapeDtypeStruct((batch * seq, SB_WIDTH), BF16),
        scratch_shapes=[pltpu.VMEM((2, TQ, T), F32), pltpu.VMEM((TQ, LANES), F32)],
        compiler_params=_cparams(("arbitrary", "arbitrary", "arbitrary")),
        name="sb_prompt",
    )(bias, qb, ktb, vtb, _sb_nu())


def _sb_sample_kernel(pt_ref, q_ref, kn_ref, vn_ref, bias_ref, nu_ref, *rest, npage, tq):
    k_refs = rest[:npage]
    v_refs = rest[npage:2 * npage]
    o_ref = rest[2 * npage]
    pad_ref = rest[2 * npage + 1]
    T = SB_TILE
    R = SB_HEADS * tq
    q = q_ref[0]
    lane = lax.broadcasted_iota(jnp.int32, (tq, SB_WIDTH), 1)
    qbd = jnp.concatenate(
        [jnp.where(_shr(lane, 6) == h, q, 0.0) for h in range(SB_HEADS)],
        axis=0).astype(BF16)
    bias = bias_ref[...]
    nu = nu_ref[...]

    def padded(ref):
        pad_ref[...] = jnp.zeros((T, SB_WIDTH), F32)
        pad_ref[0:tq, :] = ref[0]
        return pad_ref[...].astype(BF16)

    kn = padded(kn_ref)
    vn = padded(vn_ref)
    kt = jnp.concatenate([r[...].astype(BF16) for r in k_refs], axis=1)
    vt = jnp.concatenate([r[...].astype(BF16) for r in v_refs], axis=1)
    z_all = jnp.concatenate([_dot_nt(qbd, kn), _dot(qbd, kt)], axis=1)
    key = lax.broadcasted_iota(jnp.int32, (R, T), 1)
    qt = lax.broadcasted_iota(jnp.int32, (R, T), 0) & (tq - 1)
    new_mask = key < qt

    carry = jnp.zeros((R, T), F32)
    weights = []
    for t in range(npage + 1):
        z = z_all[:, t * T:(t + 1) * T] + bias
        sp = _softplus(z)
        mask = new_mask if t == 0 else None
        if mask is not None:
            sp = jnp.where(mask, sp, 0.0)
        hi, lo = _split2(sp)
        a, carry = _sb_weights(z, hi, lo, nu, carry, mask)
        weights.append(a)
    acc = _dot(weights[0], vn) + _dot_nt(jnp.concatenate(weights[1:], axis=1), vt)

    out = jnp.zeros((tq, SB_WIDTH), F32)
    for h in range(SB_HEADS):
        out = out + jnp.where(_shr(lane, 6) == h, acc[h * tq:(h + 1) * tq, :], 0.0)
    o_ref[0] = out.astype(BF16)


def _sb_sample(q8, kn8, vn8, bias, cache_kt, cache_vt, page_table, layer):
    b, tq, _ = q8.shape
    npage = page_table.shape[1]
    R = SB_HEADS * tq
    bias_tile = jnp.broadcast_to(jnp.repeat(bias.astype(F32), tq)[:, None], (R, SB_TILE))
    tok = pl.BlockSpec((1, tq, SB_WIDTH), lambda s, pt: (s, 0, 0))

    def page_spec(i):
        return pl.BlockSpec((None, None, SB_WIDTH, PAGE_SIZE),
                            lambda s, pt: (layer, pt[s, npage - 1 - i], 0, 0))

    grid_spec = pltpu.PrefetchScalarGridSpec(
        num_scalar_prefetch=1,
        grid=(b,),
        in_specs=[tok, tok, tok,
                  pl.BlockSpec((R, SB_TILE), lambda s, pt: (0, 0)),
                  pl.BlockSpec((2 * SB_TILE, 2 * SB_TILE), lambda s, pt: (0, 0))]
                 + [page_spec(i) for i in range(npage)]
                 + [page_spec(i) for i in range(npage)],
        out_specs=pl.BlockSpec((1, tq, SB_WIDTH), lambda s, pt: (s, 0, 0)),
        scratch_shapes=[pltpu.VMEM((SB_TILE, SB_WIDTH), F32)],
    )
    return pl.pallas_call(
        functools.partial(_sb_sample_kernel, npage=npage, tq=tq),
        grid_spec=grid_spec,
        out_shape=jax.ShapeDtypeStruct((b, tq, SB_WIDTH), BF16),
        compiler_params=_cparams(("arbitrary",)),
        name="sb_sample",
    )(page_table, q8, kn8, vn8, bias_tile, _sb_nu(),
      *([cache_kt] * npage), *([cache_vt] * npage))


def _out_proj_kernel(og_ref, op_ref, os_ref, x_ref, w_ref, g_ref, b_ref, o_ref, *, alpha):
    mix = (_dot(og_ref[...], w_ref[0:GLA_VAL, :])
           + _dot(op_ref[...], w_ref[GLA_VAL:GLA_VAL + POOL_WIDTH, :])
           + _dot(os_ref[...], w_ref[GLA_VAL + POOL_WIDTH:, :]))
    o_ref[...] = _layer_norm(alpha * x_ref[...] + mix, g_ref[...], b_ref[...])


def _out_proj(og, op, os_, x, w, layer, g, b, alpha):
    n, d = x.shape
    tm = min(n, 512)
    row = lambda c: pl.BlockSpec((tm, c), lambda i: (i, 0))
    const = lambda r, c: pl.BlockSpec((r, c), lambda i: (0, 0))
    return pl.pallas_call(
        functools.partial(_out_proj_kernel, alpha=alpha),
        grid=(n // tm,),
        in_specs=[row(GLA_VAL), row(POOL_WIDTH), row(SB_WIDTH), row(d),
                  pl.BlockSpec((None, w.shape[1], d), lambda i: (layer, 0, 0)),
                  const(1, d), const(1, d)],
        out_specs=row(d),
        out_shape=jax.ShapeDtypeStruct((n, d), F32),
        compiler_params=_cparams(("arbitrary",)),
        name="out_proj_ln",
    )(og, op, os_, x, w, g, b)


def _ffn_kernel(x_ref, wg_ref, wu_ref, wo_ref, g_ref, b_ref, o_ref, acc_ref, xb_ref, *, alpha):
    f = pl.program_id(1)

    @pl.when(f == 0)
    def _():
        acc_ref[...] = jnp.zeros_like(acc_ref)
        xb_ref[...] = x_ref[...].astype(BF16)

    xb = xb_ref[...]
    gate = _dot(xb, wg_ref[...])
    up = _dot(xb, wu_ref[...])
    acc_ref[...] += _dot((_silu(gate) * up).astype(BF16), wo_ref[...])

    @pl.when(f == pl.num_programs(1) - 1)
    def _():
        o_ref[...] = _layer_norm(alpha * x_ref[...] + acc_ref[...], g_ref[...], b_ref[...])


def _ffn(x, w_in, w_out, layer, g, b, alpha, *, tm, tf):
    n, d = x.shape
    hidden = w_out.shape[1]
    nf = hidden // tf
    return pl.pallas_call(
        functools.partial(_ffn_kernel, alpha=alpha),
        grid=(n // tm, nf),
        in_specs=[pl.BlockSpec((tm, d), lambda i, f: (i, 0)),
                  pl.BlockSpec((None, d, tf), lambda i, f: (layer, 0, f)),
                  pl.BlockSpec((None, d, tf), lambda i, f: (layer, 0, nf + f)),
                  pl.BlockSpec((None, tf, d), lambda i, f: (layer, f, 0)),
                  pl.BlockSpec((1, d), lambda i, f: (0, 0)),
                  pl.BlockSpec((1, d), lambda i, f: (0, 0))],
        out_specs=pl.BlockSpec((tm, d), lambda i, f: (i, 0)),
        out_shape=jax.ShapeDtypeStruct((n, d), F32),
        scratch_shapes=[pltpu.VMEM((tm, d), F32), pltpu.VMEM((tm, d), BF16)],
        compiler_params=_cparams(("arbitrary", "arbitrary")),
        name="ffn_ln",
    )(x, w_in, w_in, w_out, g, b)


def _gla_state_from_bd(bd):
    b = bd.shape[0]
    s = bd.reshape(b, GLA_HEADS, GLA_DK, GLA_HEADS, GLA_DV)
    return jnp.stack([s[:, h, :, h, :] for h in range(GLA_HEADS)], axis=1)


def kernel(x_prompt, x_sample, state_gla, state_pool, cache_k, cache_v, page_table,
           w_in, w_gate_up, b_gate, gla_norm_g, w_pool, pool_scale, sb_bias, w_out,
           ln1_g, ln1_b, w_ffn_in, w_ffn_out, ln2_g, ln2_b):
    depth = w_in.shape[0]
    batch, seq, d_model = x_prompt.shape
    dec_b, dec_t, _ = x_sample.shape
    n_phys = cache_k.shape[1]
    past_len = page_table.shape[1] * PAGE_SIZE
    alpha = (2.0 * depth) ** 0.25
    assert cache_k.shape[2] == PAGE_SIZE and seq % 512 == 0
    assert dec_t <= SUBLANES and dec_b % LANES == 0
    t_pad = SUBLANES

    w_in_t = jnp.transpose(w_in, (0, 2, 1))
    o = 0
    rows = {}
    for name, size in (("q", GLA_KEY), ("k", GLA_KEY), ("v", GLA_VAL), ("a", GATE_RANK),
                       ("g", GLA_VAL), ("u", POOL_WIDTH), ("qs", SB_WIDTH),
                       ("ks", SB_WIDTH), ("vs", SB_WIDTH)):
        rows[name] = w_in_t[:, o:o + size, :]
        o += size
    a_pad = jnp.zeros((depth, LANES - GATE_RANK, d_model), w_in.dtype)
    w_in_p = jnp.concatenate(
        [rows["q"], rows["k"], rows["v"], rows["g"], rows["a"], a_pad,
         rows["u"], rows["qs"], rows["ks"], rows["vs"]], axis=1).astype(BF16)
    wg_p = jnp.pad(w_gate_up, ((0, 0), (0, LANES - GATE_RANK), (0, 0))).astype(BF16)
    wg_t = jnp.transpose(wg_p, (0, 2, 1))
    gla_state_t = jnp.transpose(state_gla, (0, 2, 3, 4, 1)).reshape(
        depth, GLA_KEY * GLA_DV, dec_b)
    ngrp = len(POOL_WINDOWS)
    eye_g = jnp.eye(ngrp, dtype=bool)[None, :, None, :, None]
    w_pool_bd = jnp.where(eye_g, w_pool[:, :, :, None, :], 0.0).reshape(
        depth, POOL_WIDTH, POOL_WIDTH).astype(BF16)
    w_out_b = w_out.astype(BF16)
    w_ffn_in_b = w_ffn_in.astype(BF16)
    w_ffn_out_b = w_ffn_out.astype(BF16)
    cache_kt = jnp.transpose(cache_k, (0, 1, 3, 4, 2)).reshape(depth, n_phys, SB_WIDTH, PAGE_SIZE)
    cache_vt = jnp.transpose(cache_v, (0, 1, 3, 4, 2)).reshape(depth, n_phys, SB_WIDTH, PAGE_SIZE)
    pool_prev = jnp.transpose(state_pool, (0, 2, 1, 3))
    row2 = lambda a: a.reshape(1, -1).astype(F32)

    xp = x_prompt.reshape(batch * seq, d_model)
    xs = jnp.transpose(x_sample, (1, 0, 2)).reshape(dec_t * dec_b, d_model)
    zeros_state = jnp.zeros((batch, GLA_KEY, GLA_VAL), F32)
    outs = {k: [] for k in ("gla_p", "pool_p", "gla_s", "pool_s")}
    kv_stack_p = kv_stack_s = None

    def seq_major(a):
        a = jnp.transpose(a.reshape(dec_t, dec_b, a.shape[-1]), (1, 0, 2))
        return jnp.pad(a, ((0, 0), (0, t_pad - dec_t), (0, 0)))

    def tok_major(a):
        return jnp.transpose(a[:, :dec_t, :], (1, 0, 2)).reshape(dec_t * dec_b, a.shape[-1])

    for l in range(depth):
        bg, gain = row2(b_gate[l]), row2(gla_norm_g[l])
        pscale = row2(pool_scale[l])
        g1, b1, g2, b2 = row2(ln1_g[l]), row2(ln1_b[l]), row2(ln2_g[l]), row2(ln2_b[l])

        gla_in, u, qb, *kv_stack_p, ktb, vtb = _in_proj(xp, w_in_p, l, depth, kv_stack_p,
                                                        groups=batch, tm=512, prompt=True)
        o_gla, s_bd = _gla(gla_in.reshape(batch, seq, GLA_IN), zeros_state, wg_p, l, bg, gain,
                           nseq=batch, nsub=4)
        o_pool, pbuf = _pool_prompt(u.reshape(batch, seq, POOL_WIDTH), w_pool_bd, l, pscale,
                                    tt=512)
        o_sb = _sb_prompt(qb, ktb, vtb, sb_bias[l].astype(F32), seq)
        xp = _out_proj(o_gla.reshape(batch * seq, GLA_VAL), o_pool.reshape(batch * seq, POOL_WIDTH),
                       o_sb, xp, w_out_b, l, g1, b1, alpha)
        xp = _ffn(xp, w_ffn_in_b, w_ffn_out_b, l, g2, b2, alpha, tm=1024, tf=256)
        outs["gla_p"].append(_gla_state_from_bd(s_bd))
        outs["pool_p"].append(pbuf)

        gla_in, u, qb, *kv_stack_s, k, v = _in_proj(xs, w_in_p, l, depth, kv_stack_s,
                                                    groups=dec_t, tm=dec_b, prompt=False)
        o_gla, s_new = _gla_sample(gla_in, gla_state_t, wg_t, l, b_gate[l].reshape(-1, 1),
                                   gla_norm_g[l].reshape(-1, 1))
        o_pool, pbuf = _pool_sample(u.reshape(dec_t, dec_b, POOL_WIDTH), pool_prev, w_pool_bd, l,
                                    pscale, pos0=past_len)
        o_sb = _sb_sample(seq_major(qb.astype(F32)), seq_major(k), seq_major(v), sb_bias[l],
                          cache_kt, cache_vt, page_table, l)
        xs = _out_proj(o_gla.reshape(dec_t * dec_b, GLA_VAL),
                       o_pool.reshape(dec_t * dec_b, POOL_WIDTH),
                       tok_major(o_sb), xs, w_out_b, l, g1, b1, alpha)
        xs = _ffn(xs, w_ffn_in_b, w_ffn_out_b, l, g2, b2, alpha, tm=512, tf=256)
        outs["gla_s"].append(s_new)
        outs["pool_s"].append(pbuf)

    st = lambda key: jnp.stack(outs[key])
    kv_p = lambda a: jnp.transpose(
        a.reshape(depth, batch, SB_HEADS, SB_DIM, seq), (0, 1, 4, 2, 3))
    kv_s = lambda a: jnp.transpose(
        a.reshape(depth, dec_t, SB_HEADS, SB_DIM, dec_b), (0, 4, 1, 2, 3))
    y_sample = jnp.transpose(xs.reshape(dec_t, dec_b, d_model), (1, 0, 2))
    return (xp.reshape(batch, seq, d_model), y_sample,
            st("gla_p"), st("pool_p"), kv_p(kv_stack_p[0]), kv_p(kv_stack_p[1]),
            jnp.transpose(st("gla_s").reshape(depth, GLA_HEADS, GLA_DK, GLA_DV, dec_b),
                          (0, 4, 1, 2, 3)),
            jnp.transpose(st("pool_s"), (0, 2, 1, 3)),
            kv_s(kv_stack_s[0]), kv_s(kv_stack_s[1]))
```

```python
import functools

import jax
import jax.numpy as jnp
from jax import lax
from jax.experimental import pallas as pl
from jax.experimental.pallas import tpu as pltpu

F32 = jnp.float32
BF16 = jnp.bfloat16

GLA_HEADS = 4
GLA_DK = 32
GLA_DV = 64
GLA_KEY = GLA_HEADS * GLA_DK
GLA_VAL = GLA_HEADS * GLA_DV
GATE_RANK = 16
GATE_TAU = 16.0
GLA_CHUNK = 64
POOL_WINDOWS = (2, 4, 8, 16)
POOL_GROUP_DIM = 64
POOL_WIDTH = 256
POOL_BUF = 15
POOL_HALO = 16
SB_HEADS = 8
SB_DIM = 64
SB_WIDTH = SB_HEADS * SB_DIM
SB_TILE = 128
SB_QBLOCK = 4 * SB_TILE
SB_MASKED = -1e30
PAGE_SIZE = 128
LN_EPS = 1e-5

LANES = 128
SUBLANES = 8
VMEM_LIMIT = 56 * 1024 * 1024

GLA_IN = GLA_KEY + GLA_KEY + GLA_VAL + GLA_VAL + LANES
C_GLA = 0
C_U = C_GLA + GLA_IN
C_Q = C_U + POOL_WIDTH
C_K = C_Q + SB_WIDTH
C_V = C_K + SB_WIDTH
C_END = C_V + SB_WIDTH


def _cparams(sem):
    return pltpu.CompilerParams(dimension_semantics=sem, vmem_limit_bytes=VMEM_LIMIT)


def _dot(a, b):
    return jnp.dot(a, b, preferred_element_type=F32)


def _dot_nt(a, b):
    return lax.dot_general(a, b, (((1,), (1,)), ((), ())), preferred_element_type=F32)


def _dot_tn(a, b):
    return lax.dot_general(a, b, (((0,), (0,)), ((), ())), preferred_element_type=F32)


def _split3(x):
    h = x.astype(BF16)
    r = x - h.astype(F32)
    m = r.astype(BF16)
    l = (r - m.astype(F32)).astype(BF16)
    return h, m, l


def _split2(x):
    h = x.astype(BF16)
    l = (x - h.astype(F32)).astype(BF16)
    return h, l


def _shr(x, n):
    return lax.shift_right_logical(x, jnp.int32(n))


def _log_sigmoid(z):
    return jnp.minimum(z, 0.0) - jnp.log(1.0 + jnp.exp(-jnp.abs(z)))


def _silu(x):
    return x / (1.0 + jnp.exp(-x))


def _layer_norm(y, g, b):
    mu = jnp.mean(y, axis=-1, keepdims=True)
    d = y - mu
    var = jnp.mean(d * d, axis=-1, keepdims=True)
    return d * lax.rsqrt(var + LN_EPS) * g + b


def _in_proj_kernel(x_ref, w_ref, *refs, prompt, aliased):
    if aliased:
        refs = refs[2:]
    gla_ref, u_ref, qb_ref, kt_ref, vt_ref, *extra = refs
    xb = x_ref[...].astype(BF16)
    nat = lambda lo, hi: _dot_nt(xb, w_ref[lo:hi, :])
    gla_ref[...] = nat(C_GLA, C_U) if prompt else _dot_nt(w_ref[C_GLA:C_U, :], xb)
    u_ref[...] = nat(C_U, C_Q)
    qb_ref[...] = (nat(C_Q, C_K) * (SB_DIM ** -0.5)).astype(BF16)
    kt = _dot_nt(w_ref[C_K:C_V, :], xb)
    vt = _dot_nt(w_ref[C_V:C_END, :], xb)
    if aliased:
        kt_ref[...] = kt
        vt_ref[...] = vt
    else:
        for ref, val in ((kt_ref, kt), (vt_ref, vt)):
            ref[0] = val
            ref[1:] = jnp.zeros((ref.shape[0] - 1,) + val.shape, F32)
    if prompt:
        ktb_ref, vtb_ref = extra
        ktb_ref[...] = kt.astype(BF16)
        vtb_ref[...] = vt.astype(BF16)
    else:
        k_ref, v_ref = extra
        k_ref[...] = nat(C_K, C_V)
        v_ref[...] = nat(C_V, C_END)


def _in_proj(x, w_t, layer, depth, kv_stack, *, groups, tm, prompt):
    n, d = x.shape
    glen = n // groups
    per = glen // tm
    row = lambda c: pl.BlockSpec((tm, c), lambda i: (i, 0))
    ft = pl.BlockSpec((None, SB_WIDTH, tm), lambda i: (i // per, 0, i % per))
    ft_shape = lambda dt: jax.ShapeDtypeStruct((groups, SB_WIDTH, glen), dt)
    if kv_stack is None:
        stack = pl.BlockSpec((depth, None, SB_WIDTH, tm), lambda i: (0, i // per, 0, i % per))
    else:
        stack = pl.BlockSpec((None, None, SB_WIDTH, tm),
                             lambda i: (layer, i // per, 0, i % per))
    stack_shape = jax.ShapeDtypeStruct((depth, groups, SB_WIDTH, glen), F32)
    nat_shape = lambda c, dt: jax.ShapeDtypeStruct((n, c), dt)
    out_specs = [row(GLA_IN), row(POOL_WIDTH), row(SB_WIDTH), stack, stack]
    out_shape = [nat_shape(GLA_IN, F32), nat_shape(POOL_WIDTH, F32),
                 nat_shape(SB_WIDTH, BF16), stack_shape, stack_shape]
    if prompt:
        out_specs += [ft, ft]
        out_shape += [ft_shape(BF16), ft_shape(BF16)]
    else:
        out_specs[0] = pl.BlockSpec((None, GLA_IN, tm), lambda i: (i // per, 0, i % per))
        out_shape[0] = jax.ShapeDtypeStruct((groups, GLA_IN, glen), F32)
        out_specs += [row(SB_WIDTH), row(SB_WIDTH)]
        out_shape += [nat_shape(SB_WIDTH, F32), nat_shape(SB_WIDTH, F32)]
    in_specs = [row(d), pl.BlockSpec((None, C_END, d), lambda i: (layer, 0, 0))]
    operands = [x, w_t]
    aliases = {}
    if kv_stack is not None:
        in_specs += [pl.BlockSpec(memory_space=pl.ANY)] * 2
        operands += list(kv_stack)
        aliases = {2: 3, 3: 4}
    return pl.pallas_call(
        functools.partial(_in_proj_kernel, prompt=prompt, aliased=kv_stack is not None),
        grid=(n // tm,),
        in_specs=in_specs,
        out_specs=out_specs,
        out_shape=out_shape,
        input_output_aliases=aliases,
        compiler_params=_cparams(("arbitrary",)),
        name="in_proj",
    )(*operands)


def _gla_kernel(x_ref, s0_ref, wg_ref, bg_ref, gain_ref, tri_ref, hmean_ref,
                o_ref, s_ref, *, nseq, nsub):
    L = GLA_CHUNK
    c = pl.program_id(1)

    @pl.when(c == 0)
    def _():
        s_ref[...] = s0_ref[...]

    lane_k = lax.broadcasted_iota(jnp.int32, (L, GLA_KEY), 1)
    tq = lax.broadcasted_iota(jnp.int32, (GLA_HEADS * L, L), 0)
    ts = lax.broadcasted_iota(jnp.int32, (GLA_HEADS * L, L), 1)
    causal = (tq & (L - 1)) >= ts
    lane_v = lax.broadcasted_iota(jnp.int32, (L, GLA_VAL), 1)
    srow = lax.broadcasted_iota(jnp.int32, (GLA_KEY, GLA_VAL), 0)
    scol = lax.broadcasted_iota(jnp.int32, (GLA_KEY, GLA_VAL), 1)
    sdiag = _shr(srow, 5) == _shr(scol, 6)
    tri = tri_ref[...]
    hmean = hmean_ref[...]

    items = [(i, sub) for i in range(nseq) for sub in range(nsub)]
    dot3 = lambda x, w: (lambda h, m, l: _dot(h, w) + _dot(m, w) + _dot(l, w))(*_split3(x))

    rows_of = [slice(sub * L, (sub + 1) * L) for _, sub in items]
    blks = [x_ref[i, rows, :] for (i, _), rows in zip(items, rows_of)]
    q =[blk[:, 0:GLA_KEY] * (GLA_DK ** -0.5) for blk in blks]
    k = [blk[:, GLA_KEY:2 * GLA_KEY] for blk in blks]
    vb = [blk[:, 2 * GLA_KEY:2 * GLA_KEY + GLA_VAL].astype(BF16) for blk in blks]
    g = [blk[:, 2 * GLA_KEY + GLA_VAL:2 * GLA_KEY + 2 * GLA_VAL] for blk in blks]

    z = [_dot(blk[:, 2 * GLA_KEY + 2 * GLA_VAL:GLA_IN].astype(BF16), wg_ref[...]) + bg_ref[...]
         for blk in blks]
    log_a = [_log_sigmoid(zi) * (1.0 / GATE_TAU) for zi in z]
    b =[(lambda h, m, l: _dot(tri, h) + _dot(tri, m) + _dot(tri, l))(*_split3(la))
         for la in log_a]
    b_last = [bi[L - 1:L, :] for bi in b]
    b_mid = [bi[L // 2 - 1:L // 2, :] for bi in b]

    def stack_heads(qm):
        return jnp.concatenate(
            [jnp.where(_shr(lane_k, 5) == h, qm, 0.0) for h in range(GLA_HEADS)],
            axis=0).astype(BF16)

    scores = [jnp.where(causal,
                        _dot_nt(stack_heads(q[n] * jnp.exp(b[n] - b_mid[n])),
                                (k[n] * jnp.exp(b_mid[n] - b[n])).astype(BF16)), 0.0)
              for n in range(len(items))]
    upd = [jnp.where(sdiag, _dot_tn((k[n] * jnp.exp(b_last[n] - b[n])).astype(BF16), vb[n]), 0.0)
           for n in range(len(items))]
    p = [_dot(scores[n].astype(BF16), vb[n]) for n in range(len(items))]

    o = []
    for n, (i, sub) in enumerate(items):
        if sub == 0:
            s_bd = s_ref[i]
        o.append(_dot((q[n] * jnp.exp(b[n])).astype(BF16), s_bd.astype(BF16)))
        dcol = jnp.transpose(jnp.broadcast_to(jnp.exp(b_last[n]), (GLA_KEY, GLA_KEY)))
        s_bd = jnp.concatenate([dcol, dcol], axis=1) * s_bd + upd[n]
        if sub == nsub - 1:
            s_ref[i] = s_bd
    for n in range(len(items)):
        for h in range(GLA_HEADS):
            o[n] = o[n] + jnp.where(_shr(lane_v, 6) == h, p[n][h * L:(h + 1) * L, :], 0.0)

    d = [o[n] - dot3(o[n], hmean) for n in range(len(items))]
    var = [dot3(dn * dn, hmean) for dn in d]
    for n, (i, sub) in enumerate(items):
        y = d[n] * lax.rsqrt(var[n] + LN_EPS) * gain_ref[...] * _silu(g[n])
        o_ref[i, rows_of[n], :] = y.astype(BF16)


def _gla(x, s0_bd, wg, layer, bg, gain, *, nseq, nsub):
    b, t, _ = x.shape
    L = GLA_CHUNK
    t_blk = nsub * L
    assert t % t_blk == 0 and b % nseq == 0
    nchunk = t // t_blk
    tri = (lax.broadcasted_iota(jnp.int32, (L, L), 0)
           >= lax.broadcasted_iota(jnp.int32, (L, L), 1)).astype(BF16)
    hm = ((lax.broadcasted_iota(jnp.int32, (GLA_VAL, GLA_VAL), 0) // GLA_DV)
          == (lax.broadcasted_iota(jnp.int32, (GLA_VAL, GLA_VAL), 1) // GLA_DV))
    hmean = jnp.where(hm, 1.0 / GLA_DV, 0.0).astype(BF16)
    const = lambda shape: pl.BlockSpec(shape, lambda s, c: (0,) * len(shape))
    return pl.pallas_call(
        functools.partial(_gla_kernel, nseq=nseq, nsub=nsub),
        grid=(b // nseq, nchunk),
        in_specs=[pl.BlockSpec((nseq, t_blk, GLA_IN), lambda s, c: (s, c, 0)),
                  pl.BlockSpec((nseq, GLA_KEY, GLA_VAL), lambda s, c: (s, 0, 0)),
                  pl.BlockSpec((None, LANES, GLA_KEY), lambda s, c: (layer, 0, 0)),
                  const((1, GLA_KEY)), const((1, GLA_VAL)),
                  const((L, L)), const((GLA_VAL, GLA_VAL))],
        out_specs=[pl.BlockSpec((nseq, t_blk, GLA_VAL), lambda s, c: (s, c, 0)),
                   pl.BlockSpec((nseq, GLA_KEY, GLA_VAL), lambda s, c: (s, 0, 0))],
        out_shape=[jax.ShapeDtypeStruct((b, t, GLA_VAL), BF16),
                   jax.ShapeDtypeStruct((b, GLA_KEY, GLA_VAL), F32)],
        compiler_params=_cparams(("arbitrary", "arbitrary")),
        name="gla",
    )(x, s0_bd, wg, bg, gain, tri, hmean)


def _gla_sample_kernel(g_ref, s_ref, wg_ref, bg_ref, gain_ref, o_ref, so_ref,
                       dec_scr, q_scr, y_scr, *, nt):
    B = g_ref.shape[-1]
    v0 = 2 * GLA_KEY
    g0 = v0 + GLA_VAL
    a0 = g0 + GLA_VAL
    for t in range(nt):
        z = _dot(wg_ref[...], g_ref[t, a0:GLA_IN, :].astype(BF16)) + bg_ref[...]
        dec_scr[t] = jnp.exp(_log_sigmoid(z) * (1.0 / GATE_TAU))
        q_scr[t] = g_ref[t, 0:GLA_KEY, :] * (GLA_DK ** -0.5)

    for h in range(GLA_HEADS):
        def body(kk, o, h=h):
            r = h * GLA_DK + kk
            rows = pl.ds(pl.multiple_of(r * GLA_DV, GLA_DV), GLA_DV)
            s = s_ref[rows, :]
            o = list(o)
            for t in range(nt):
                v = g_ref[t, v0 + h * GLA_DV:v0 + (h + 1) * GLA_DV, :]
                s = dec_scr[t, pl.ds(r, 1), :] * s + g_ref[t, pl.ds(GLA_KEY + r, 1), :] * v
                o[t] = o[t] + q_scr[t, pl.ds(r, 1), :] * s
            so_ref[rows, :] = s
            return tuple(o)

        o = lax.fori_loop(0, GLA_DK, body,
                          tuple(jnp.zeros((GLA_DV, B), F32) for _ in range(nt)))
        hs = slice(h * GLA_DV, (h + 1) * GLA_DV)
        for t in range(nt):
            mu = jnp.mean(o[t], axis=0, keepdims=True)
            d = o[t] - mu
            var = jnp.mean(d * d, axis=0, keepdims=True)
            gate = g_ref[t, g0 + h * GLA_DV:g0 + (h + 1) * GLA_DV, :]
            y_scr[t, hs, :] = d * lax.rsqrt(var + LN_EPS) * gain_ref[hs, :] * _silu(gate)
    for t in range(nt):
        o_ref[t] = jnp.transpose(y_scr[t]).astype(BF16)


def _gla_sample(g_t, state_all, wg_t, layer, bg_col, gain_col):
    nt, _, b = g_t.shape
    ns = GLA_KEY * GLA_DV
    return pl.pallas_call(
        functools.partial(_gla_sample_kernel, nt=nt),
        grid=(1,),
        in_specs=[pl.BlockSpec((nt, GLA_IN, b), lambda i: (0, 0, 0)),
                  pl.BlockSpec((None, ns, b), lambda i: (layer, 0, 0)),
                  pl.BlockSpec((None, GLA_KEY, LANES), lambda i: (layer, 0, 0)),
                  pl.BlockSpec((GLA_KEY, 1), lambda i: (0, 0)),
                  pl.BlockSpec((GLA_VAL, 1), lambda i: (0, 0))],
        out_specs=[pl.BlockSpec((nt, b, GLA_VAL), lambda i: (0, 0, 0)),
                   pl.BlockSpec((ns, b), lambda i: (0, 0))],
        out_shape=[jax.ShapeDtypeStruct((nt, b, GLA_VAL), BF16),
                   jax.ShapeDtypeStruct((ns, b), F32)],
        scratch_shapes=[pltpu.VMEM((nt, GLA_KEY, b), F32), pltpu.VMEM((nt, GLA_KEY, b), F32),
                        pltpu.VMEM((nt, GLA_VAL, b), F32)],
        compiler_params=_cparams(("arbitrary",)),
        name="gla_sample",
    )(g_t, state_all, wg_t, bg_col, gain_col)


def _pool_window(x0, sh, grp):
    s2 = x0 + sh(1)
    s4 = s2 + sh(2) + sh(3)
    s8 = s4 + sh(4) + sh(5) + sh(6) + sh(7)
    s16 = s8 + sh(8) + sh(9) + sh(10) + sh(11) + sh(12) + sh(13) + sh(14) + sh(15)
    return jnp.where(grp == 0, s2, jnp.where(grp == 1, s4, jnp.where(grp == 2, s8, s16)))


def _pool_width(grp):
    return jnp.where(grp == 0, POOL_WINDOWS[0],
                     jnp.where(grp == 1, POOL_WINDOWS[1],
                               jnp.where(grp == 2, POOL_WINDOWS[2], POOL_WINDOWS[3])))


def _pool_prompt_kernel(u_ref, w_ref, scale_ref, o_ref, buf_ref, x_scr, *, tt):
    t = pl.program_id(1)
    H = POOL_HALO

    @pl.when(t == 0)
    def _():
        x_scr[0:H, :] = jnp.zeros((H, POOL_WIDTH), F32)

    x_scr[H:H + tt, :] = u_ref[...]
    x0 = x_scr[H:H + tt, :]
    shape = (tt, POOL_WIDTH)
    grp = _shr(lax.broadcasted_iota(jnp.int32, shape, 1), 6)
    win = _pool_window(x0, lambda j: x_scr[H - j:H - j + tt, :], grp)
    pos = t * tt + lax.broadcasted_iota(jnp.int32, shape, 0)
    cnt = jnp.minimum(pos + 1, _pool_width(grp)).astype(F32)
    d = win / cnt - x0
    o_ref[...] = (_dot(d.astype(BF16), w_ref[...]) * scale_ref[...]).astype(BF16)

    @pl.when(t == pl.num_programs(1) - 1)
    def _():
        buf_ref[...] = x_scr[H + tt - POOL_BUF:H + tt, :]

    x_scr[0:H, :] = x_scr[tt:tt + H, :]


def _pool_prompt(u, w_bd, layer, scale, *, tt):
    b, t, w = u.shape
    return pl.pallas_call(
        functools.partial(_pool_prompt_kernel, tt=tt),
        grid=(b, t // tt),
        in_specs=[pl.BlockSpec((None, tt, w), lambda s, i: (s, i, 0)),
                  pl.BlockSpec((None, w, w), lambda s, i: (layer, 0, 0)),
                  pl.BlockSpec((1, w), lambda s, i: (0, 0))],
        out_specs=[pl.BlockSpec((None, tt, w), lambda s, i: (s, i, 0)),
                   pl.BlockSpec((None, POOL_BUF, w), lambda s, i: (s, 0, 0))],
        out_shape=[jax.ShapeDtypeStruct((b, t, w), BF16),
                   jax.ShapeDtypeStruct((b, POOL_BUF, w), F32)],
        scratch_shapes=[pltpu.VMEM((POOL_HALO + tt, w), F32)],
        compiler_params=_cparams(("arbitrary", "arbitrary")),
        name="pool_prompt",
    )(u, w_bd, scale)


def _pool_sample_kernel(u_ref, prev_ref, w_ref, scale_ref, o_ref, buf_ref, *, nt, pos0):
    ext = [prev_ref[i] for i in range(POOL_BUF)] + [u_ref[t] for t in range(nt)]
    shape = ext[0].shape
    grp = _shr(lax.broadcasted_iota(jnp.int32, shape, 1), 6)
    width = _pool_width(grp)
    for t in range(nt):
        x0 = ext[POOL_BUF + t]
        win = _pool_window(x0, lambda j: ext[POOL_BUF + t - j], grp)
        cnt = jnp.minimum(pos0 + t + 1, width).astype(F32)
        d = win / cnt - x0
        o_ref[t] = (_dot(d.astype(BF16), w_ref[...]) * scale_ref[...]).astype(BF16)
    for i in range(POOL_BUF):
        buf_ref[i] = ext[nt + i]


def _pool_sample(u, prev_all, w_bd, layer, scale, *, pos0):
    nt, b, w = u.shape
    return pl.pallas_call(
        functools.partial(_pool_sample_kernel, nt=nt, pos0=pos0),
        grid=(1,),
        in_specs=[pl.BlockSpec((nt, b, w), lambda i: (0, 0, 0)),
                  pl.BlockSpec((None, POOL_BUF, b, w), lambda i: (layer, 0, 0, 0)),
                  pl.BlockSpec((None, w, w), lambda i: (layer, 0, 0)),
                  pl.BlockSpec((1, w), lambda i: (0, 0))],
        out_specs=[pl.BlockSpec((nt, b, w), lambda i: (0, 0, 0)),
                   pl.BlockSpec((POOL_BUF, b, w), lambda i: (0, 0, 0))],
        out_shape=[jax.ShapeDtypeStruct((nt, b, w), BF16),
                   jax.ShapeDtypeStruct((POOL_BUF, b, w), F32)],
        compiler_params=_cparams(("arbitrary",)),
        name="pool_sample",
    )(u, prev_all, w_bd, scale)


def _softplus(z):
    neg_abs = lax.bitcast_convert_type(
        lax.bitcast_convert_type(z, jnp.uint32) | jnp.uint32(0x80000000), F32)
    return jnp.maximum(z, 0.0) + jnp.log(1.0 + jnp.exp(neg_abs))


def _sb_weights(z_t, hi_t, lo_t, nu, carry, mask):
    return _sb_weights_from(z_t, _sb_suffix(hi_t, lo_t, nu), carry, mask)


def _sb_suffix(hi_t, lo_t, nu):
    return _dot(jnp.concatenate([hi_t, lo_t], axis=1), nu)


def _sb_weights_from(z_t, er, carry, mask):
    x = z_t + er[:, :SB_TILE] + carry
    if mask is not None:
        x = jnp.where(mask, x, SB_MASKED)
    return jnp.exp(x).astype(BF16), carry + er[:, SB_TILE:]


def _sb_nu():
    r = lax.broadcasted_iota(jnp.int32, (2 * SB_TILE, 2 * SB_TILE), 0) & (SB_TILE - 1)
    c = lax.broadcasted_iota(jnp.int32, (2 * SB_TILE, 2 * SB_TILE), 1)
    return jnp.where((c >= SB_TILE) | (r >= c), -1.0, 0.0).astype(BF16)


def _sb_prompt_kernel(bias_ref, q_ref, k_ref, v_ref, nu_ref, o_ref, carry_ref, acc_ref):
    pair = pl.program_id(1)
    qi = pl.program_id(2)
    T = SB_TILE
    TQ = SB_QBLOCK
    q = q_ref[...]
    nu = nu_ref[...]
    col = lax.broadcasted_iota(jnp.int32, (LANES, 4 * T), 1)
    brow = lax.broadcasted_iota(jnp.int32, (LANES, 4 * T), 0)
    bias_full = jnp.where((_shr(col, 7) & 1) == 0, bias_ref[2 * pair], bias_ref[2 * pair + 1])
    b_h, b_m, b_l = [p.astype(F32) for p in _split3(bias_full)]
    bias_rows = jnp.where(brow == 0, b_h, jnp.where(brow == 1, b_m,
                                                    jnp.where(brow == 2, b_l, 0.0))).astype(BF16)
    qlane = lax.broadcasted_iota(jnp.int32, (TQ, LANES), 1)
    q = jnp.concatenate([q, jnp.where(qlane < 3, 1.0, 0.0).astype(BF16)], axis=1)
    head0 = lax.broadcasted_iota(jnp.int32, (LANES, T), 0) < SB_DIM
    colq = lax.broadcasted_iota(jnp.int32, (TQ, 4 * T), 1)
    key_local = (colq & (T - 1)) + jnp.where(colq < 2 * T, T, 0)
    rowq = lax.broadcasted_iota(jnp.int32, (TQ, 4 * T), 0)
    ndiag = TQ // (2 * T)

    def per_head(x):
        zero = jnp.zeros_like(x)
        return [jnp.where(head0, x, zero), jnp.where(head0, zero, x)]

    def run(blocks, r0=0):
        rows = slice(r0, TQ)
        zs, v_rhss, diags = [], [], []
        for kb, diag_index in blocks:
            start = pl.multiple_of(kb * 2 * T, 2 * T)
            kt = k_ref[:, pl.ds(start, 2 * T)]
            vt = v_ref[:, pl.ds(start, 2 * T)]
            k_rhs = jnp.concatenate(per_head(kt[:, T:]) + per_head(kt[:, :T]), axis=1)
            v_rhss.append(jnp.concatenate(per_head(vt[:, T:]) + per_head(vt[:, :T]), axis=1))
            zs.append(_dot(q[rows], jnp.concatenate([k_rhs, bias_rows], axis=0)))
            diags.append(None if diag_index is None
                         else (key_local + diag_index * 2 * T < rowq)[rows])
        tiles = [slice(c * T, (c + 1) * T) for c in range(4)]
        ers = []
        for z, diag in zip(zs, diags):
            sp = _softplus(z)
            if diag is not None:
                sp = jnp.where(diag, sp, 0.0)
            hi, lo = _split2(sp)
            ers.append([_sb_suffix(hi[:, sl], lo[:, sl], nu) for sl in tiles])
        carries = [carry_ref[0, rows, :], carry_ref[1, rows, :]]
        acc = acc_ref[rows, :]
        for z, diag, er, v_rhs in zip(zs, diags, ers, v_rhss):
            weights = []
            for c, sl in enumerate(tiles):
                a, carries[c % 2] = _sb_weights_from(
                    z[:, sl], er[c], carries[c % 2], None if diag is None else diag[:, sl])
                weights.append(a)
            acc = acc + _dot_nt(jnp.concatenate(weights, axis=1), v_rhs)
        carry_ref[0, rows, :] = carries[0]
        carry_ref[1, rows, :] = carries[1]
        acc_ref[rows, :] = acc

    carry_ref[...] = jnp.zeros_like(carry_ref)
    acc_ref[...] = jnp.zeros_like(acc_ref)
    for d in reversed(range(ndiag)):
        run([(qi * ndiag + d, d)], r0=d * 2 * T)

    def body(s, c):
        first = qi * ndiag - 1 - s * ndiag
        run([(first - u, None) for u in range(ndiag)])
        return c

    lax.fori_loop(0, qi, body, 0)
    o_ref[...] = acc_ref[...].astype(BF16)


def _sb_prompt(qb, ktb, vtb, bias, seq):
    T = SB_TILE
    TQ = SB_QBLOCK
    batch = ktb.shape[0]
    nq = seq // TQ
    return pl.pallas_call(
        _sb_prompt_kernel,
        grid=(batch, SB_HEADS // 2, nq),
        in_specs=[pl.BlockSpec(memory_space=pltpu.SMEM),
                  pl.BlockSpec((TQ, LANES), lambda b, p, i: (b * nq + i, p)),
                  pl.BlockSpec((None, LANES, seq), lambda b, p, i: (b, p, 0)),
                  pl.BlockSpec((None, LANES, seq), lambda b, p, i: (b, p, 0)),
                  pl.BlockSpec((2 * T, 2 * T), lambda b, p, i: (0, 0))],
        out_specs=pl.BlockSpec((TQ, LANES), lambda b, p, i: (b * nq + i, p)),
        out_shape=jax.ShapeDtypeStruct((batch * seq, SB_WIDTH), BF16),
        scratch_shapes=[pltpu.VMEM((2, TQ, T), F32), pltpu.VMEM((TQ, LANES), F32)],
        compiler_params=_cparams(("arbitrary", "arbitrary", "arbitrary")),
        name="sb_prompt",
    )(bias, qb, ktb, vtb, _sb_nu())


def _sb_sample_kernel(pt_ref, q_ref, kn_ref, vn_ref, bias_ref, nu_ref, *rest, npage, tq):
    k_refs = rest[:npage]
    v_refs = rest[npage:2 * npage]
    o_ref = rest[2 * npage]
    pad_ref = rest[2 * npage + 1]
    T = SB_TILE
    R = SB_HEADS * tq
    q = q_ref[0]
    lane = lax.broadcasted_iota(jnp.int32, (tq, SB_WIDTH), 1)
    qbd = jnp.concatenate(
        [jnp.where(_shr(lane, 6) == h, q, 0.0) for h in range(SB_HEADS)],
        axis=0).astype(BF16)
    bias = bias_ref[...]
    nu = nu_ref[...]

    def padded(ref):
        pad_ref[...] = jnp.zeros((T, SB_WIDTH), F32)
        pad_ref[0:tq, :] = ref[0]
        return pad_ref[...].astype(BF16)

    kn = padded(kn_ref)
    vn = padded(vn_ref)
    kt = jnp.concatenate([r[...].astype(BF16) for r in k_refs], axis=1)
    vt = jnp.concatenate([r[...].astype(BF16) for r in v_refs], axis=1)
    z_all = jnp.concatenate([_dot_nt(qbd, kn), _dot(qbd, kt)], axis=1)
    key = lax.broadcasted_iota(jnp.int32, (R, T), 1)
    qt = lax.broadcasted_iota(jnp.int32, (R, T), 0) & (tq - 1)
    new_mask = key < qt

    carry = jnp.zeros((R, T), F32)
    weights = []
    for t in range(npage + 1):
        z = z_all[:, t * T:(t + 1) * T] + bias
        sp = _softplus(z)
        mask = new_mask if t == 0 else None
        if mask is not None:
            sp = jnp.where(mask, sp, 0.0)
        hi, lo = _split2(sp)
        a, carry = _sb_weights(z, hi, lo, nu, carry, mask)
        weights.append(a)
    acc = _dot(weights[0], vn) + _dot_nt(jnp.concatenate(weights[1:], axis=1), vt)

    out = jnp.zeros((tq, SB_WIDTH), F32)
    for h in range(SB_HEADS):
        out = out + jnp.where(_shr(lane, 6) == h, acc[h * tq:(h + 1) * tq, :], 0.0)
    o_ref[0] = out.astype(BF16)


def _sb_sample(q8, kn8, vn8, bias, cache_kt, cache_vt, page_table, layer):
    b, tq, _ = q8.shape
    npage = page_table.shape[1]
    R = SB_HEADS * tq
    bias_tile = jnp.broadcast_to(jnp.repeat(bias.astype(F32), tq)[:, None], (R, SB_TILE))
    tok = pl.BlockSpec((1, tq, SB_WIDTH), lambda s, pt: (s, 0, 0))

    def page_spec(i):
        return pl.BlockSpec((None, None, SB_WIDTH, PAGE_SIZE),
                            lambda s, pt: (layer, pt[s, npage - 1 - i], 0, 0))

    grid_spec = pltpu.PrefetchScalarGridSpec(
        num_scalar_prefetch=1,
        grid=(b,),
        in_specs=[tok, tok, tok,
                  pl.BlockSpec((R, SB_TILE), lambda s, pt: (0, 0)),
                  pl.BlockSpec((2 * SB_TILE, 2 * SB_TILE), lambda s, pt: (0, 0))]
                 + [page_spec(i) for i in range(npage)]
                 + [page_spec(i) for i in range(npage)],
        out_specs=pl.BlockSpec((1, tq, SB_WIDTH), lambda s, pt: (s, 0, 0)),
        scratch_shapes=[pltpu.VMEM((SB_TILE, SB_WIDTH), F32)],
    )
    return pl.pallas_call(
        functools.partial(_sb_sample_kernel, npage=npage, tq=tq),
        grid_spec=grid_spec,
        out_shape=jax.ShapeDtypeStruct((b, tq, SB_WIDTH), BF16),
        compiler_params=_cparams(("arbitrary",)),
        name="sb_sample",
    )(page_table, q8, kn8, vn8, bias_tile, _sb_nu(),
      *([cache_kt] * npage), *([cache_vt] * npage))


def _out_proj_kernel(og_ref, op_ref, os_ref, x_ref, w_ref, g_ref, b_ref, o_ref, *, alpha):
    mix = (_dot(og_ref[...], w_ref[0:GLA_VAL, :].astype(BF16))
           + _dot(op_ref[...], w_ref[GLA_VAL:GLA_VAL + POOL_WIDTH, :].astype(BF16))
           + _dot(os_ref[...], w_ref[GLA_VAL + POOL_WIDTH:, :].astype(BF16)))
    o_ref[...] = _layer_norm(alpha * x_ref[...] + mix, g_ref[...], b_ref[...])


def _out_proj(og, op, os_, x, w, layer, g, b, alpha):
    n, d = x.shape
    tm = min(n, 512)
    row = lambda c: pl.BlockSpec((tm, c), lambda i: (i, 0))
    const = lambda r, c: pl.BlockSpec((r, c), lambda i: (0, 0))
    return pl.pallas_call(
        functools.partial(_out_proj_kernel, alpha=alpha),
        grid=(n // tm,),
        in_specs=[row(GLA_VAL), row(POOL_WIDTH), row(SB_WIDTH), row(d),
                  pl.BlockSpec((None, w.shape[1], d), lambda i: (layer, 0, 0)),
                  const(1, d), const(1, d)],
        out_specs=row(d),
        out_shape=jax.ShapeDtypeStruct((n, d), F32),
        compiler_params=_cparams(("arbitrary",)),
        name="out_proj_ln",
    )(og, op, os_, x, w, g, b)


def _ffn_kernel(x_ref, wg_ref, wu_ref, wo_ref, g_ref, b_ref, o_ref, acc_ref, xb_ref, *, alpha):
    f = pl.program_id(1)

    @pl.when(f == 0)
    def _():
        acc_ref[...] = jnp.zeros_like(acc_ref)
        xb_ref[...] = x_ref[...].astype(BF16)

    xb = xb_ref[...]
    gate = _dot(xb, wg_ref[...].astype(BF16))
    up = _dot(xb, wu_ref[...].astype(BF16))
    acc_ref[...] += _dot((_silu(gate) * up).astype(BF16), wo_ref[...].astype(BF16))

    @pl.when(f == pl.num_programs(1) - 1)
    def _():
        o_ref[...] = _layer_norm(alpha * x_ref[...] + acc_ref[...], g_ref[...], b_ref[...])


def _ffn(x, w_in, w_out, layer, g, b, alpha, *, tm, tf):
    n, d = x.shape
    hidden = w_out.shape[1]
    nf = hidden // tf
    return pl.pallas_call(
        functools.partial(_ffn_kernel, alpha=alpha),
        grid=(n // tm, nf),
        in_specs=[pl.BlockSpec((tm, d), lambda i, f: (i, 0)),
                  pl.BlockSpec((None, d, tf), lambda i, f: (layer, 0, f)),
                  pl.BlockSpec((None, d, tf), lambda i, f: (layer, 0, nf + f)),
                  pl.BlockSpec((None, tf, d), lambda i, f: (layer, f, 0)),
                  pl.BlockSpec((1, d), lambda i, f: (0, 0)),
                  pl.BlockSpec((1, d), lambda i, f: (0, 0))],
        out_specs=pl.BlockSpec((tm, d), lambda i, f: (i, 0)),
        out_shape=jax.ShapeDtypeStruct((n, d), F32),
        scratch_shapes=[pltpu.VMEM((tm, d), F32), pltpu.VMEM((tm, d), BF16)],
        compiler_params=_cparams(("arbitrary", "arbitrary")),
        name="ffn_ln",
    )(x, w_in, w_in, w_out, g, b)


def _gla_state_from_bd(bd):
    b = bd.shape[0]
    s = bd.reshape(b, GLA_HEADS, GLA_DK, GLA_HEADS, GLA_DV)
    return jnp.stack([s[:, h, :, h, :] for h in range(GLA_HEADS)], axis=1)


def kernel(x_prompt, x_sample, state_gla, state_pool, cache_k, cache_v, page_table,
           w_in, w_gate_up, b_gate, gla_norm_g, w_pool, pool_scale, sb_bias, w_out,
           ln1_g, ln1_b, w_ffn_in, w_ffn_out, ln2_g, ln2_b):
    depth = w_in.shape[0]
    batch, seq, d_model = x_prompt.shape
    dec_b, dec_t, _ = x_sample.shape
    n_phys = cache_k.shape[1]
    past_len = page_table.shape[1] * PAGE_SIZE
    alpha = (2.0 * depth) ** 0.25
    assert cache_k.shape[2] == PAGE_SIZE and seq % 512 == 0
    assert dec_t <= SUBLANES and dec_b % LANES == 0
    t_pad = SUBLANES

    w_in_t = jnp.transpose(w_in, (0, 2, 1))
    o = 0
    rows = {}
    for name, size in (("q", GLA_KEY), ("k", GLA_KEY), ("v", GLA_VAL), ("a", GATE_RANK),
                       ("g", GLA_VAL), ("u", POOL_WIDTH), ("qs", SB_WIDTH),
                       ("ks", SB_WIDTH), ("vs", SB_WIDTH)):
        rows[name] = w_in_t[:, o:o + size, :]
        o += size
    a_pad = jnp.zeros((depth, LANES - GATE_RANK, d_model), w_in.dtype)
    w_in_p = jnp.concatenate(
        [rows["q"], rows["k"], rows["v"], rows["g"], rows["a"], a_pad,
         rows["u"], rows["qs"], rows["ks"], rows["vs"]], axis=1).astype(BF16)
    wg_p = jnp.pad(w_gate_up, ((0, 0), (0, LANES - GATE_RANK), (0, 0))).astype(BF16)
    wg_t = jnp.transpose(wg_p, (0, 2, 1))
    gla_state_t = jnp.transpose(state_gla, (0, 2, 3, 4, 1)).reshape(
        depth, GLA_KEY * GLA_DV, dec_b)
    ngrp = len(POOL_WINDOWS)
    eye_g = jnp.eye(ngrp, dtype=bool)[None, :, None, :, None]
    w_pool_bd = jnp.where(eye_g, w_pool[:, :, :, None, :], 0.0).reshape(
        depth, POOL_WIDTH, POOL_WIDTH).astype(BF16)
    w_out_b, w_ffn_in_b, w_ffn_out_b = w_out, w_ffn_in, w_ffn_out
    cache_kt = jnp.transpose(cache_k, (0, 1, 3, 4, 2)).reshape(depth, n_phys, SB_WIDTH, PAGE_SIZE)
    cache_vt = jnp.transpose(cache_v, (0, 1, 3, 4, 2)).reshape(depth, n_phys, SB_WIDTH, PAGE_SIZE)
    pool_prev = jnp.transpose(state_pool, (0, 2, 1, 3))
    row2 = lambda a: a.reshape(1, -1).astype(F32)

    xp = x_prompt.reshape(batch * seq, d_model)
    xs = jnp.transpose(x_sample, (1, 0, 2)).reshape(dec_t * dec_b, d_model)
    zeros_state = jnp.zeros((batch, GLA_KEY, GLA_VAL), F32)
    outs = {k: [] for k in ("gla_p", "pool_p", "gla_s", "pool_s")}
    kv_stack_p = kv_stack_s = None

    def seq_major(a):
        a = jnp.transpose(a.reshape(dec_t, dec_b, a.shape[-1]), (1, 0, 2))
        return jnp.pad(a, ((0, 0), (0, t_pad - dec_t), (0, 0)))

    def tok_major(a):
        return jnp.transpose(a[:, :dec_t, :], (1, 0, 2)).reshape(dec_t * dec_b, a.shape[-1])

    for l in range(depth):
        bg, gain = row2(b_gate[l]), row2(gla_norm_g[l])
        pscale = row2(pool_scale[l])
        g1, b1, g2, b2 = row2(ln1_g[l]), row2(ln1_b[l]), row2(ln2_g[l]), row2(ln2_b[l])

        gla_in, u, qb, *kv_stack_p, ktb, vtb = _in_proj(xp, w_in_p, l, depth, kv_stack_p,
                                                        groups=batch, tm=512, prompt=True)
        o_gla, s_bd = _gla(gla_in.reshape(batch, seq, GLA_IN), zeros_state, wg_p, l, bg, gain,
                           nseq=batch, nsub=8)
        o_pool, pbuf = _pool_prompt(u.reshape(batch, seq, POOL_WIDTH), w_pool_bd, l, pscale,
                                    tt=512)
        o_sb = _sb_prompt(qb, ktb, vtb, sb_bias[l].astype(F32), seq)
        xp = _out_proj(o_gla.reshape(batch * seq, GLA_VAL), o_pool.reshape(batch * seq, POOL_WIDTH),
                       o_sb, xp, w_out_b, l, g1, b1, alpha)
        xp = _ffn(xp, w_ffn_in_b, w_ffn_out_b, l, g2, b2, alpha, tm=1024, tf=256)
        outs["gla_p"].append(_gla_state_from_bd(s_bd))
        outs["pool_p"].append(pbuf)

        gla_in, u, qb, *kv_stack_s, k, v = _in_proj(xs, w_in_p, l, depth, kv_stack_s,
                                                    groups=dec_t, tm=dec_b, prompt=False)
        o_gla, s_new = _gla_sample(gla_in, gla_state_t, wg_t, l, b_gate[l].reshape(-1, 1),
                                   gla_norm_g[l].reshape(-1, 1))
        o_pool, pbuf = _pool_sample(u.reshape(dec_t, dec_b, POOL_WIDTH), pool_prev, w_pool_bd, l,
                                    pscale, pos0=past_len)
        o_sb = _sb_sample(seq_major(qb.astype(F32)), seq_major(k), seq_major(v), sb_bias[l],
                          cache_kt, cache_vt, page_table, l)
        xs = _out_proj(o_gla.reshape(dec_t * dec_b, GLA_VAL),
                       o_pool.reshape(dec_t * dec_b, POOL_WIDTH),
                       tok_major(o_sb), xs, w_out_b, l, g1, b1, alpha)
        xs = _ffn(xs, w_ffn_in_b, w_ffn_out_b, l, g2, b2, alpha, tm=512, tf=256)
        outs["gla_s"].append(s_new)
        outs["pool_s"].append(pbuf)

    st = lambda key: jnp.stack(outs[key])
    kv_p = lambda a: jnp.transpose(
        a.reshape(depth, batch, SB_HEADS, SB_DIM, seq), (0, 1, 4, 2, 3))
    kv_s = lambda a: jnp.transpose(
        a.reshape(depth, dec_t, SB_HEADS, SB_DIM, dec_b), (0, 4, 1, 2, 3))
    y_sample = jnp.transpose(xs.reshape(dec_t, dec_b, d_model), (1, 0, 2))
    return (xp.reshape(batch, seq, d_model), y_sample,
            st("gla_p"), st("pool_p"), kv_p(kv_stack_p[0]), kv_p(kv_stack_p[1]),
            jnp.transpose(st("gla_s").reshape(depth, GLA_HEADS, GLA_DK, GLA_DV, dec_b),
                          (0, 4, 1, 2, 3)),
            jnp.transpose(st("pool_s"), (0, 2, 1, 3)),
            kv_s(kv_stack_s[0]), kv_s(kv_stack_s[1]))
```
